```python
import jax, jax.numpy as jnp
from jax import lax
import numpy as np

D_MODEL = 1024
BATCH = 8
SEQ = 4096
DEPTH = 1

HEAD_DIM = 64
RWKV_HEADS = 8
RWKV_WIDTH = RWKV_HEADS * HEAD_DIM
ATT_Q_HEADS = 8
ATT_KV_HEADS = 2
ATT_GROUP = ATT_Q_HEADS // ATT_KV_HEADS
ATT_WIDTH = ATT_Q_HEADS * HEAD_DIM
KV_WIDTH = ATT_KV_HEADS * HEAD_DIM
WINDOW = 128
BLOCK = 128
DECAY_LORA = 32
ICLR_LORA = 32
GATE_LORA = 96
D_FF = 2816
N_BRANCH = 2
RMS_EPS = 1e-6
GN_EPS = 64e-5

RWKV_COLS = 3 * RWKV_WIDTH + DECAY_LORA + ICLR_LORA + GATE_LORA
ATT_COLS = ATT_WIDTH + 2 * KV_WIDTH
GATE_COLS = N_BRANCH * D_MODEL
IN_COLS = RWKV_COLS + ATT_COLS + GATE_COLS

kernel_name = "rwkv7_swa_sink_macaron_hybrid"


def rms_norm(x, g, eps=RMS_EPS):
    xf = x.astype(jnp.float32)
    y = xf * lax.rsqrt(jnp.mean(xf * xf, axis=-1, keepdims=True) + eps)
    return (y * g.astype(jnp.float32)).astype(x.dtype)


def swiglu(h, w_gate, w_up, w_down):
    return (jax.nn.silu(h @ w_gate) * (h @ w_up)) @ w_down


def token_shift(p):
    return jnp.pad(p, ((0, 0), (1, 0), (0, 0)))[:, :-1]


def wkv7_scan(r, w, k, v, a, b):
    Bsz, T, H, N = r.shape
    xs = tuple(jnp.moveaxis(t, 1, 0) for t in (r, w, k, v, a, b))

    def step(S, inp):
        r_t, w_t, k_t, v_t, a_t, b_t = inp
        sa = jnp.einsum('bhij,bhj->bhi', S, a_t)
        S = S * w_t[:, :, None, :] + sa[..., None] * b_t[:, :, None, :] + v_t[..., None] * k_t[:, :, None, :]
        y_t = jnp.einsum('bhij,bhj->bhi', S, r_t)
        return S, y_t

    S0 = jnp.zeros((Bsz, H, N, N), jnp.float32)
    _, ys = lax.scan(step, S0, xs)
    return jnp.moveaxis(ys, 0, 1)


def rwkv7_branch(p, mu, w0, w_lora_up, a0, a_lora_up, g_lora_up, k_k, k_a, r_k, ln_w, ln_b):
    Bsz, T, _ = p.shape
    f32 = jnp.float32
    p = p + (token_shift(p) - p) * mu
    r, k, v, xw, xa, xg = jnp.split(
        p, [RWKV_WIDTH, 2 * RWKV_WIDTH, 3 * RWKV_WIDTH,
            3 * RWKV_WIDTH + DECAY_LORA, 3 * RWKV_WIDTH + DECAY_LORA + ICLR_LORA], axis=-1)
    w_log = -jax.nn.softplus(-(w0 + jnp.tanh(xw) @ w_lora_up)) - 0.5
    decay = jnp.exp(-jnp.exp(w_log.astype(f32)))
    a = jax.nn.sigmoid(a0 + xa @ a_lora_up)
    g = jax.nn.sigmoid(xg) @ g_lora_up

    def heads(t):
        return t.reshape(Bsz, T, RWKV_HEADS, HEAD_DIM).astype(f32)

    r, k, v, decay, a = heads(r), heads(k), heads(v), heads(decay), heads(a)
    k_k = k_k.reshape(RWKV_HEADS, HEAD_DIM).astype(f32)
    k_a = k_a.reshape(RWKV_HEADS, HEAD_DIM).astype(f32)
    kk = k * k_k
    kk = kk / jnp.maximum(jnp.sqrt(jnp.sum(kk * kk, axis=-1, keepdims=True)), 1e-12)
    k = k * (1.0 + (a - 1.0) * k_a)
    y = wkv7_scan(r, decay, k, v, -kk, kk * a)
    mean = jnp.mean(y, axis=-1, keepdims=True)
    var = jnp.mean(jnp.square(y - mean), axis=-1, keepdims=True)
    y = (y - mean) * lax.rsqrt(var + GN_EPS)
    y = y * ln_w.reshape(RWKV_HEADS, HEAD_DIM).astype(f32) + ln_b.reshape(RWKV_HEADS, HEAD_DIM).astype(f32)
    bonus = jnp.sum(r * k * r_k.astype(f32), axis=-1, keepdims=True) * v
    y = (y + bonus).reshape(Bsz, T, RWKV_WIDTH) * g.astype(f32)
    return y.astype(p.dtype)


def sliding_window_attention(q, k, v, sinks):
    Bsz, T = q.shape[:2]
    nb = T // BLOCK
    f32 = jnp.float32
    qb = q.reshape(Bsz, nb, BLOCK, ATT_KV_HEADS, ATT_GROUP, HEAD_DIM)

    def with_prev(t):
        tb = t.reshape(Bsz, nb, BLOCK, ATT_KV_HEADS, HEAD_DIM)
        prev = jnp.pad(tb, ((0, 0), (1, 0), (0, 0), (0, 0), (0, 0)))[:, :-1]
        return jnp.concatenate([prev, tb], axis=2)

    kc, vc = with_prev(k), with_prev(v)
    scale = HEAD_DIM ** -0.5
    s = jnp.einsum('bnqhgd,bnkhd->bnhgqk', qb, kc).astype(f32) * scale
    qi = jnp.arange(BLOCK)[:, None]
    kj = jnp.arange(2 * BLOCK)[None, :]
    band = (kj <= qi + BLOCK) & (kj > qi + BLOCK - WINDOW)
    valid = (jnp.arange(nb)[:, None, None] > 0) | (kj >= BLOCK)[None]
    mask = (band[None] & valid)[None, :, None, None]
    s = jnp.where(mask, s, -jnp.inf)
    sink = sinks.astype(f32).reshape(1, 1, ATT_KV_HEADS, ATT_GROUP, 1, 1)
    m = jnp.maximum(jnp.max(s, axis=-1, keepdims=True), sink)
    pexp = jnp.exp(s - m)
    denom = jnp.sum(pexp, axis=-1, keepdims=True) + jnp.exp(sink - m)
    probs = (pexp / denom).astype(v.dtype)
    o = jnp.einsum('bnhgqk,bnkhd->bnqhgd', probs, vc)
    return o.reshape(Bsz, T, ATT_WIDTH)


def attention_branch(p, q_norm, k_norm, sinks):
    Bsz, T, _ = p.shape
    q, k, v = jnp.split(p, [ATT_WIDTH, ATT_WIDTH + KV_WIDTH], axis=-1)
    q = rms_norm(q.reshape(Bsz, T, ATT_Q_HEADS, HEAD_DIM), q_norm)
    k = rms_norm(k.reshape(Bsz, T, ATT_KV_HEADS, HEAD_DIM), k_norm)
    v = v.reshape(Bsz, T, ATT_KV_HEADS, HEAD_DIM)
    return sliding_window_attention(q, k, v, sinks)


def setup_inputs(seed: int = 0) -> dict:
    key = jax.random.key(seed)
    ks = iter(jax.random.split(key, 40))
    L = DEPTH

    def nrm(shape, scale):
        return jax.random.normal(next(ks), shape, jnp.float32) * scale

    def gain(shape):
        return 1.0 + nrm(shape, 0.02)

    return {
        "x": jax.random.normal(next(ks), (BATCH, SEQ, D_MODEL), jnp.float32),
        "ffn1_norm": gain((L, D_MODEL)),
        "ffn1_w_gate": nrm((L, D_MODEL, D_FF), D_MODEL ** -0.5),
        "ffn1_w_up": nrm((L, D_MODEL, D_FF), D_MODEL ** -0.5),
        "ffn1_w_down": nrm((L, D_FF, D_MODEL), D_FF ** -0.5),
        "mix_norm": gain((L, D_MODEL)),
        "w_in": nrm((L, D_MODEL, IN_COLS), D_MODEL ** -0.5),
        "rwkv_mu": jax.random.uniform(next(ks), (L, RWKV_COLS), jnp.float32),
        "rwkv_w0": jax.random.uniform(next(ks), (L, RWKV_WIDTH), jnp.float32, -6.5, -1.5),
        "rwkv_w_lora_up": nrm((L, DECAY_LORA, RWKV_WIDTH), 0.1),
        "rwkv_a0": nrm((L, RWKV_WIDTH), 0.1),
        "rwkv_a_lora_up": nrm((L, ICLR_LORA, RWKV_WIDTH), 0.1),
        "rwkv_g_lora_up": nrm((L, GATE_LORA, RWKV_WIDTH), GATE_LORA ** -0.5),
        "rwkv_k_k": 0.85 + nrm((L, RWKV_WIDTH), 0.02),
        "rwkv_k_a": gain((L, RWKV_WIDTH)),
        "rwkv_r_k": nrm((L, RWKV_HEADS, HEAD_DIM), 0.1),
        "rwkv_ln_w": gain((L, RWKV_WIDTH)),
        "rwkv_ln_b": nrm((L, RWKV_WIDTH), 0.02),
        "attn_q_norm": gain((L, HEAD_DIM)),
        "attn_k_norm": gain((L, HEAD_DIM)),
        "attn_sinks": nrm((L, ATT_Q_HEADS), 0.5),
        "w_branch_rwkv": nrm((L, RWKV_WIDTH, D_MODEL), RWKV_WIDTH ** -0.5),
        "w_branch_attn": nrm((L, ATT_WIDTH, D_MODEL), ATT_WIDTH ** -0.5),
        "w_out": nrm((L, D_MODEL, D_MODEL), D_MODEL ** -0.5),
        "ffn2_norm": gain((L, D_MODEL)),
        "ffn2_w_gate": nrm((L, D_MODEL, D_FF), D_MODEL ** -0.5),
        "ffn2_w_up": nrm((L, D_MODEL, D_FF), D_MODEL ** -0.5),
        "ffn2_w_down": nrm((L, D_FF, D_MODEL), D_FF ** -0.5),
        "final_norm": gain((L, D_MODEL)),
    }


def reference(x, ffn1_norm, ffn1_w_gate, ffn1_w_up, ffn1_w_down, mix_norm, w_in,
              rwkv_mu, rwkv_w0, rwkv_w_lora_up, rwkv_a0, rwkv_a_lora_up, rwkv_g_lora_up,
              rwkv_k_k, rwkv_k_a, rwkv_r_k, rwkv_ln_w, rwkv_ln_b,
              attn_q_norm, attn_k_norm, attn_sinks,
              w_branch_rwkv, w_branch_attn, w_out,
              ffn2_norm, ffn2_w_gate, ffn2_w_up, ffn2_w_down, final_norm):
    for l in range(DEPTH):
        x = x + 0.5 * swiglu(rms_norm(x, ffn1_norm[l]), ffn1_w_gate[l], ffn1_w_up[l], ffn1_w_down[l])
        h = rms_norm(x, mix_norm[l])
        proj = h @ w_in[l]
        p_rwkv, p_att, p_gate = jnp.split(proj, [RWKV_COLS, RWKV_COLS + ATT_COLS], axis=-1)
        y_rwkv = rwkv7_branch(p_rwkv, rwkv_mu[l], rwkv_w0[l], rwkv_w_lora_up[l], rwkv_a0[l],
                              rwkv_a_lora_up[l], rwkv_g_lora_up[l], rwkv_k_k[l], rwkv_k_a[l],
                              rwkv_r_k[l], rwkv_ln_w[l], rwkv_ln_b[l])
        y_att = attention_branch(p_att, attn_q_norm[l], attn_k_norm[l], attn_sinks[l])
        gate_rwkv, gate_att = jnp.split(jax.nn.sigmoid(p_gate), N_BRANCH, axis=-1)
        merged = gate_rwkv * (y_rwkv @ w_branch_rwkv[l]) + gate_att * (y_att @ w_branch_attn[l])
        x = x + merged @ w_out[l]
        x = x + 0.5 * swiglu(rms_norm(x, ffn2_norm[l]), ffn2_w_gate[l], ffn2_w_up[l], ffn2_w_down[l])
        x = rms_norm(x, final_norm[l])
    return x
```

```python
import functools

import jax
import jax.numpy as jnp
from jax import lax
from jax.experimental import pallas as pl
from jax.experimental.pallas import tpu as pltpu

F32 = jnp.float32
BF16 = jnp.bfloat16

HEAD_DIM = 64
RWKV_HEADS = 8
RWKV_WIDTH = RWKV_HEADS * HEAD_DIM
ATT_Q_HEADS = 8
ATT_KV_HEADS = 2
ATT_GROUP = ATT_Q_HEADS // ATT_KV_HEADS
ATT_WIDTH = ATT_Q_HEADS * HEAD_DIM
KV_WIDTH = ATT_KV_HEADS * HEAD_DIM
WINDOW = 128
DECAY_LORA = 32
ICLR_LORA = 32
GATE_LORA = 96
LORA_PAD = 256
RMS_EPS = 1e-6
GN_EPS = 64e-5
CHUNK = 64
EXP_M05 = 0.6065306597126334
NEG_BIG = -1e30

VMEM_LIMIT = 56 * 1024 * 1024


def _sigmoid(x):
    return 1.0 / (1.0 + jnp.exp(-x))


def _split(a):
    hi = a.astype(BF16)
    lo = (a - hi.astype(F32)).astype(BF16)
    return hi, lo


def _mm(a, b, dims, passes):
    dn = (dims, ((), ()))
    if passes == 1:
        return lax.dot_general(a.astype(BF16), b.astype(BF16), dn, preferred_element_type=F32)
    ah, al = _split(a)
    bh, bl = _split(b)
    out = lax.dot_general(ah, bh, dn, preferred_element_type=F32)
    out += lax.dot_general(ah, bl, dn, preferred_element_type=F32)
    out += lax.dot_general(al, bh, dn, preferred_element_type=F32)
    return out


NN = ((1,), (0,))
NT = ((1,), (1,))
TN = ((0,), (0,))


def _mm_exact_rhs(a, b_bf16, passes=2):
    dn = (NN, ((), ()))
    out = None
    rem = a
    for _ in range(passes):
        part = rem.astype(BF16)
        term = lax.dot_general(part, b_bf16, dn, preferred_element_type=F32)
        out = term if out is None else out + term
        rem = rem - part.astype(F32)
    return out


def _mm_exact_lhs(a_bf16, b, passes=3):
    dn = (NN, ((), ()))
    out = None
    rem = b
    for _ in range(passes):
        part = rem.astype(BF16)
        term = lax.dot_general(a_bf16, part, dn, preferred_element_type=F32)
        out = term if out is None else out + term
        rem = rem - part.astype(F32)
    return out


def _ffn_kernel(x_ref, g_ref, wg_ref, wu_ref, wd_ref, *rest, nf, final):
    if final:
        fg_ref, o_ref, h_ref, acc_ref = rest
    else:
        o_ref, h_ref, acc_ref = rest
    j = pl.program_id(1)

    @pl.when(j == 0)
    def _():
        x = x_ref[...]
        ms = jnp.mean(x * x, axis=-1, keepdims=True)
        h_ref[...] = (x * lax.rsqrt(ms + RMS_EPS) * g_ref[...]).astype(BF16)
        acc_ref[...] = jnp.zeros_like(acc_ref)

    h = h_ref[...]
    g = jnp.dot(h, wg_ref[...], preferred_element_type=F32)
    u = jnp.dot(h, wu_ref[...], preferred_element_type=F32)
    act = (g * _sigmoid(g) * u).astype(BF16)
    acc_ref[...] += jnp.dot(act, wd_ref[...], preferred_element_type=F32)

    @pl.when(j == nf - 1)
    def _():
        y = x_ref[...] + 0.5 * acc_ref[...]
        if final:
            ms = jnp.mean(y * y, axis=-1, keepdims=True)
            y = y * lax.rsqrt(ms + RMS_EPS) * fg_ref[...]
        o_ref[...] = y


def _ffn(x, gain, wg, wu, wd, final_gain=None, *, tm=512, tf=1408):
    n, d = x.shape
    dff = wg.shape[1]
    nf = dff // tf
    final = final_gain is not None
    in_specs = [
        pl.BlockSpec((tm, d), lambda i, j: (i, 0)),
        pl.BlockSpec((1, d), lambda i, j: (0, 0)),
        pl.BlockSpec((d, tf), lambda i, j: (0, j)),
        pl.BlockSpec((d, tf), lambda i, j: (0, j)),
        pl.BlockSpec((tf, d), lambda i, j: (j, 0)),
    ]
    args = [x, gain.reshape(1, d), wg, wu, wd]
    if final:
        in_specs.append(pl.BlockSpec((1, d), lambda i, j: (0, 0)))
        args.append(final_gain.reshape(1, d))
    return pl.pallas_call(
        functools.partial(_ffn_kernel, nf=nf, final=final),
        grid=(n // tm, nf),
        in_specs=in_specs,
        out_specs=pl.BlockSpec((tm, d), lambda i, j: (i, 0)),
        out_shape=jax.ShapeDtypeStruct((n, d), F32),
        scratch_shapes=[pltpu.VMEM((tm, d), BF16), pltpu.VMEM((tm, d), F32)],
        compiler_params=pltpu.CompilerParams(
            dimension_semantics=("parallel", "arbitrary"), vmem_limit_bytes=VMEM_LIMIT),
        name="ffn_final" if final else "ffn",
    )(*args)


def _proj_kernel(x_ref, g_ref, w_ref, rkv_ref, lora_ref, att_ref, gate_ref, *, cols):
    x = x_ref[...]
    ms = jnp.mean(x * x, axis=-1, keepdims=True)
    h = (x * lax.rsqrt(ms + RMS_EPS) * g_ref[...]).astype(BF16)
    c0 = 0
    for ref, c in zip((rkv_ref, lora_ref, att_ref, gate_ref), cols):
        ref[...] = jnp.dot(h, w_ref[:, c0:c0 + c], preferred_element_type=F32)
        c0 += c


def _proj(x, gain, w, cols, *, tm=256):
    n, d = x.shape
    return pl.pallas_call(
        functools.partial(_proj_kernel, cols=cols),
        grid=(n // tm,),
        in_specs=[
            pl.BlockSpec((tm, d), lambda i: (i, 0)),
            pl.BlockSpec((1, d), lambda i: (0, 0)),
            pl.BlockSpec(w.shape, lambda i: (0, 0)),
        ],
        out_specs=[pl.BlockSpec((tm, c), lambda i: (i, 0)) for c in cols],
        out_shape=[jax.ShapeDtypeStruct((n, c), F32) for c in cols],
        compiler_params=pltpu.CompilerParams(
            dimension_semantics=("parallel",), vmem_limit_bytes=VMEM_LIMIT),
        name="proj",
    )(x, gain.reshape(1, d), w)


SCAN_PASSES = 3


def _rwkv_kernel(rkv_ref, lora_ref, mu_rkv_ref, mu_lora_ref, wl_ref, w0_ref, a0_ref, kk_ref, ka_ref,
                 rk_ref, lnw_ref, lnb_ref, seg_ref, o_ref, s_ref, prev_rkv_ref, prev_lora_ref):
    t = pl.program_id(1)
    L = rkv_ref.shape[1]
    W = RWKV_WIDTH

    @pl.when(t == 0)
    def _():
        s_ref[...] = jnp.zeros_like(s_ref)
        prev_rkv_ref[...] = jnp.zeros_like(prev_rkv_ref)
        prev_lora_ref[...] = jnp.zeros_like(prev_lora_ref)

    row = lax.broadcasted_iota(jnp.int32, (L, 1), 0)

    def lerp(p, prev_ref, mu):
        shifted = jnp.where(row == 0, prev_ref[...], pltpu.roll(p, 1, axis=0))
        prev_ref[...] = p[L - 1:L, :]
        return p + (shifted - p) * mu

    p = lerp(rkv_ref[0], prev_rkv_ref, mu_rkv_ref[...])
    pl_ = lerp(lora_ref[0], prev_lora_ref, mu_lora_ref[...])
    r = p[:, 0:W]
    k = p[:, W:2 * W]
    v = p[:, 2 * W:3 * W]

    lane = lax.broadcasted_iota(jnp.int32, (1, LORA_PAD), 1)
    act = jnp.where(lane < DECAY_LORA, jnp.tanh(pl_),
                    jnp.where(lane < DECAY_LORA + ICLR_LORA, pl_,
                              jnp.where(lane < DECAY_LORA + ICLR_LORA + GATE_LORA, _sigmoid(pl_), 0.0)))
    z = _mm(act, wl_ref[...], NN, 3)
    e = EXP_M05 * _sigmoid(z[:, 0:W] + w0_ref[...])
    a_ic = _sigmoid(z[:, W:2 * W] + a0_ref[...])
    gate = z[:, 2 * W:3 * W]

    seg = seg_ref[...]
    kk = k * kk_ref[...]
    kk = kk / jnp.maximum(jnp.sqrt(_mm_exact_rhs(kk * kk, seg)), 1e-12)
    k2 = k * (1.0 + (a_ic - 1.0) * ka_ref[...])
    av = -kk
    bv = kk * a_ic

    ci = lax.broadcasted_iota(jnp.int32, (L, L), 0)
    cj = lax.broadcasted_iota(jnp.int32, (L, L), 1)
    tri_incl = jnp.where(cj <= ci, 1.0, 0.0).astype(BF16)
    cw = -_mm_exact_lhs(tri_incl, e, 3)
    cw_last = cw[L - 1:L, :]
    at = av * jnp.exp(cw + e)
    rt = r * jnp.exp(cw)
    einv = jnp.exp(-cw)
    bt = bv * einv
    kt = k2 * einv
    eend = jnp.exp(cw_last - cw)
    bw = bv * eend
    kw = k2 * eend
    w_end = jnp.exp(cw_last)

    strict = cj < ci
    incl = cj <= ci
    eye = jnp.where(ci == cj, 1.0, 0.0)
    ps = SCAN_PASSES
    ys = []
    for h in range(RWKV_HEADS):
        sl = slice(h * HEAD_DIM, (h + 1) * HEAD_DIM)
        a_h, r_h, b_h, k_h, v_h = at[:, sl], rt[:, sl], bt[:, sl], kt[:, sl], v[:, sl]
        aab = jnp.where(strict, _mm(a_h, b_h, NT, ps), 0.0)
        aak = jnp.where(strict, _mm(a_h, k_h, NT, ps), 0.0)
        arb = jnp.where(incl, _mm(r_h, b_h, NT, ps), 0.0)
        ark = jnp.where(incl, _mm(r_h, k_h, NT, ps), 0.0)
        tm = eye + aab
        pw = aab
        for _ in range(5):
            pw = _mm(pw, pw, NN, ps)
            tm = tm + _mm(tm, pw, NN, ps)
        s = s_ref[h]
        x = _mm(a_h, s, NT, ps) + _mm(aak, v_h, NN, ps)
        u = _mm(tm, x, NN, ps)
        ys.append(_mm(r_h, s, NT, ps) + _mm(arb, u, NN, ps) + _mm(ark, v_h, NN, ps))
        s_ref[h] = (s * w_end[:, sl] + _mm(u, bw[:, sl], TN, ps) + _mm(v_h, kw[:, sl], TN, ps))
    y = jnp.concatenate(ys, axis=-1)

    inv_n = 1.0 / HEAD_DIM
    mean = _mm_exact_rhs(y, seg) * inv_n
    d = y - mean
    var = _mm_exact_rhs(d * d, seg) * inv_n
    yn = d * lax.rsqrt(var + GN_EPS) * lnw_ref[...] + lnb_ref[...]
    bonus = _mm_exact_rhs(r * k2 * rk_ref[...], seg) * v
    o_ref[0] = ((yn + bonus) * gate).astype(o_ref.dtype)


def _rwkv(p_rkv, p_lora, mu_rkv, mu_lora, w_lora, w0, a0, k_k, k_a, r_k, ln_w, ln_b, seg):
    b, t, _ = p_rkv.shape
    W = RWKV_WIDTH
    L = CHUNK
    row = lambda a: a.reshape(1, -1)
    const = lambda a: pl.BlockSpec(a.shape, lambda i, j: (0,) * a.ndim)
    consts = [row(mu_rkv), row(mu_lora), w_lora, row(w0), row(a0), row(k_k), row(k_a), row(r_k),
              row(ln_w), row(ln_b), seg]
    return pl.pallas_call(
        _rwkv_kernel,
        grid=(b, t // L),
        in_specs=[pl.BlockSpec((1, L, 3 * W), lambda i, j: (i, j, 0)),
                  pl.BlockSpec((1, L, LORA_PAD), lambda i, j: (i, j, 0))] + [const(a) for a in consts],
        out_specs=pl.BlockSpec((1, L, W), lambda i, j: (i, j, 0)),
        out_shape=jax.ShapeDtypeStruct((b, t, W), BF16),
        scratch_shapes=[pltpu.VMEM((RWKV_HEADS, HEAD_DIM, HEAD_DIM), F32),
                        pltpu.VMEM((1, 3 * W), F32), pltpu.VMEM((1, LORA_PAD), F32)],
        compiler_params=pltpu.CompilerParams(
            dimension_semantics=("parallel", "arbitrary"), vmem_limit_bytes=VMEM_LIMIT),
        name="rwkv",
    )(p_rkv, p_lora, *consts)


def _attn_kernel(sink_ref, q_ref, kvp_ref, kvc_ref, qn_ref, kn_ref, segq_ref, segk_ref, o_ref):
    blk = pl.program_id(1)
    T = q_ref.shape[1]
    inv_n = 1.0 / HEAD_DIM

    q = q_ref[0]
    q = q * lax.rsqrt(_mm_exact_rhs(q * q, segq_ref[...]) * inv_n + RMS_EPS) * qn_ref[...]

    def knorm(kv):
        kx = kv[:, 0:KV_WIDTH]
        return kx * lax.rsqrt(_mm_exact_rhs(kx * kx, segk_ref[...]) * inv_n + RMS_EPS) * kn_ref[...]

    kvp = kvp_ref[0]
    kvc = kvc_ref[0]
    kcat = jnp.concatenate([knorm(kvp), knorm(kvc)], axis=0).astype(BF16)
    vcat = jnp.concatenate([kvp[:, KV_WIDTH:], kvc[:, KV_WIDTH:]], axis=0).astype(BF16)

    qi = lax.broadcasted_iota(jnp.int32, (T, 2 * T), 0)
    kj = lax.broadcasted_iota(jnp.int32, (T, 2 * T), 1)
    mask = (kj <= qi + T) & (kj > qi + T - WINDOW) & ((blk > 0) | (kj >= T))
    scale = HEAD_DIM ** -0.5
    qb = q.astype(BF16)
    outs = []
    for h in range(ATT_Q_HEADS):
        g = h // ATT_GROUP
        qh = qb[:, h * HEAD_DIM:(h + 1) * HEAD_DIM]
        kg = kcat[:, g * HEAD_DIM:(g + 1) * HEAD_DIM]
        vg = vcat[:, g * HEAD_DIM:(g + 1) * HEAD_DIM]
        s = lax.dot_general(qh, kg, (NT, ((), ())), preferred_element_type=F32) * scale
        s = jnp.where(mask, s, NEG_BIG)
        sink = sink_ref[h]
        m = jnp.maximum(jnp.max(s, axis=-1, keepdims=True), sink)
        pe = jnp.exp(s - m)
        denom = jnp.sum(pe, axis=-1, keepdims=True) + jnp.exp(sink - m)
        probs = (pe / denom).astype(BF16)
        outs.append(jnp.dot(probs, vg, preferred_element_type=F32))
    o_ref[0] = jnp.concatenate(outs, axis=-1).astype(o_ref.dtype)


def _attn(p_att, q_norm, k_norm, sinks, segq, segk):
    b, t, _ = p_att.shape
    T = WINDOW
    nq = ATT_WIDTH // (2 * KV_WIDTH)
    const = lambda a: pl.BlockSpec(a.shape, lambda i, j: (0,) * a.ndim)
    qn = jnp.tile(q_norm, ATT_Q_HEADS).reshape(1, ATT_WIDTH)
    kn = jnp.tile(k_norm, ATT_KV_HEADS).reshape(1, KV_WIDTH)
    return pl.pallas_call(
        _attn_kernel,
        grid=(b, t // T),
        in_specs=[pl.BlockSpec(memory_space=pltpu.SMEM),
                  pl.BlockSpec((1, T, ATT_WIDTH), lambda i, j: (i, j, 0)),
                  pl.BlockSpec((1, T, 2 * KV_WIDTH), lambda i, j: (i, jnp.maximum(j - 1, 0), nq)),
                  pl.BlockSpec((1, T, 2 * KV_WIDTH), lambda i, j: (i, j, nq)),
                  const(qn), const(kn), const(segq), const(segk)],
        out_specs=pl.BlockSpec((1, T, ATT_WIDTH), lambda i, j: (i, j, 0)),
        out_shape=jax.ShapeDtypeStruct((b, t, ATT_WIDTH), BF16),
        compiler_params=pltpu.CompilerParams(
            dimension_semantics=("parallel", "parallel"), vmem_limit_bytes=VMEM_LIMIT),
        name="attn",
    )(sinks, p_att, p_att, p_att, qn, kn, segq, segk)


def _merge_kernel(x_ref, yr_ref, ya_ref, gate_ref, wr_ref, wa_ref, wo_ref, o_ref):
    d = x_ref.shape[1]
    gate = gate_ref[...]
    br = jnp.dot(yr_ref[...], wr_ref[...], preferred_element_type=F32)
    ba = jnp.dot(ya_ref[...], wa_ref[...], preferred_element_type=F32)
    merged = _sigmoid(gate[:, 0:d]) * br + _sigmoid(gate[:, d:2 * d]) * ba
    o_ref[...] = x_ref[...] + jnp.dot(merged.astype(BF16), wo_ref[...], preferred_element_type=F32)


def _merge(x, y_rwkv, y_att, p_gate, wr, wa, wo, *, tm=512):
    n, d = x.shape
    const = lambda a: pl.BlockSpec(a.shape, lambda i: (0,) * a.ndim)
    tile = lambda a: pl.BlockSpec((tm, a.shape[1]), lambda i: (i, 0))
    return pl.pallas_call(
        _merge_kernel,
        grid=(n // tm,),
        in_specs=[tile(x), tile(y_rwkv), tile(y_att), tile(p_gate), const(wr), const(wa), const(wo)],
        out_specs=tile(x),
        out_shape=jax.ShapeDtypeStruct((n, d), F32),
        compiler_params=pltpu.CompilerParams(
            dimension_semantics=("parallel",), vmem_limit_bytes=VMEM_LIMIT),
        name="merge",
    )(x, y_rwkv, y_att, p_gate, wr, wa, wo)


def _block_diag_ones(width):
    i = jnp.arange(width) // HEAD_DIM
    return (i[:, None] == i[None, :]).astype(BF16)


def _layer(x, ffn1_norm, ffn1_w_gate, ffn1_w_up, ffn1_w_down, mix_norm, w_in,
           rwkv_mu, rwkv_w0, rwkv_w_lora_up, rwkv_a0, rwkv_a_lora_up, rwkv_g_lora_up,
           rwkv_k_k, rwkv_k_a, rwkv_r_k, rwkv_ln_w, rwkv_ln_b,
           attn_q_norm, attn_k_norm, attn_sinks,
           w_branch_rwkv, w_branch_attn, w_out,
           ffn2_norm, ffn2_w_gate, ffn2_w_up, ffn2_w_down, final_norm):
    b, t, d = x.shape
    n = b * t
    W = RWKV_WIDTH
    n_lora = DECAY_LORA + ICLR_LORA + GATE_LORA
    rwkv_cols = 3 * W + n_lora
    att_cols = ATT_WIDTH + 2 * KV_WIDTH
    cols = (3 * W, LORA_PAD, att_cols, 2 * d)

    bf = lambda a: a.astype(BF16)
    x2 = x.reshape(n, d)
    x2 = _ffn(x2, ffn1_norm, bf(ffn1_w_gate), bf(ffn1_w_up), bf(ffn1_w_down))

    pad = jnp.zeros((d, LORA_PAD - n_lora), F32)
    w_in_p = bf(jnp.concatenate([w_in[:, :rwkv_cols], pad, w_in[:, rwkv_cols:]], axis=1))
    p_rkv, p_lora, p_att, p_gate = _proj(x2, mix_norm, w_in_p, cols)

    mu_rkv = rwkv_mu[:3 * W]
    mu_lora = jnp.concatenate([rwkv_mu[3 * W:], jnp.zeros((LORA_PAD - n_lora,), F32)])
    w_lora = jnp.zeros((LORA_PAD, 3 * W), F32)
    w_lora = w_lora.at[0:DECAY_LORA, 0:W].set(rwkv_w_lora_up)
    w_lora = w_lora.at[DECAY_LORA:DECAY_LORA + ICLR_LORA, W:2 * W].set(rwkv_a_lora_up)
    w_lora = w_lora.at[DECAY_LORA + ICLR_LORA:n_lora, 2 * W:3 * W].set(rwkv_g_lora_up)
    y_rwkv = _rwkv(p_rkv.reshape(b, t, 3 * W), p_lora.reshape(b, t, LORA_PAD), mu_rkv, mu_lora, w_lora,
                   rwkv_w0, rwkv_a0, rwkv_k_k, rwkv_k_a, rwkv_r_k.reshape(-1), rwkv_ln_w, rwkv_ln_b,
                   _block_diag_ones(W))
    y_att = _attn(p_att.reshape(b, t, att_cols), attn_q_norm, attn_k_norm, attn_sinks,
                  _block_diag_ones(ATT_WIDTH), _block_diag_ones(KV_WIDTH))

    x2 = _merge(x2, y_rwkv.reshape(n, W), y_att.reshape(n, ATT_WIDTH), p_gate,
                bf(w_branch_rwkv), bf(w_branch_attn), bf(w_out))
    x2 = _ffn(x2, ffn2_norm, bf(ffn2_w_gate), bf(ffn2_w_up), bf(ffn2_w_down), final_norm)
    return x2.reshape(b, t, d)


def kernel(x, ffn1_norm, ffn1_w_gate, ffn1_w_up, ffn1_w_down, mix_norm, w_in, rwkv_mu, rwkv_w0, rwkv_w_lora_up, rwkv_a0, rwkv_a_lora_up, rwkv_g_lora_up, rwkv_k_k, rwkv_k_a, rwkv_r_k, rwkv_ln_w, rwkv_ln_b, attn_q_norm, attn_k_norm, attn_sinks, w_branch_rwkv, w_branch_attn, w_out, ffn2_norm, ffn2_w_gate, ffn2_w_up, ffn2_w_down, final_norm):
    params = (ffn1_norm, ffn1_w_gate, ffn1_w_up, ffn1_w_down, mix_norm, w_in, rwkv_mu, rwkv_w0,
              rwkv_w_lora_up, rwkv_a0, rwkv_a_lora_up, rwkv_g_lora_up, rwkv_k_k, rwkv_k_a, rwkv_r_k,
              rwkv_ln_w, rwkv_ln_b, attn_q_norm, attn_k_norm, attn_sinks, w_branch_rwkv, w_branch_attn,
              w_out, ffn2_norm, ffn2_w_gate, ffn2_w_up, ffn2_w_down, final_norm)
    for layer in range(ffn1_norm.shape[0]):
        x = _layer(x, *(p[layer] for p in params))
    return x
```

```python
import functools

import jax
import jax.numpy as jnp
from jax import lax
from jax.experimental import pallas as pl
from jax.experimental.pallas import tpu as pltpu

F32 = jnp.float32
BF16 = jnp.bfloat16

HEAD_DIM = 64
RWKV_HEADS = 8
RWKV_WIDTH = RWKV_HEADS * HEAD_DIM
ATT_Q_HEADS = 8
ATT_KV_HEADS = 2
ATT_GROUP = ATT_Q_HEADS // ATT_KV_HEADS
ATT_WIDTH = ATT_Q_HEADS * HEAD_DIM
KV_WIDTH = ATT_KV_HEADS * HEAD_DIM
WINDOW = 128
DECAY_LORA = 32
ICLR_LORA = 32
GATE_LORA = 96
LORA_PAD = 256
RMS_EPS = 1e-6
GN_EPS = 64e-5
CHUNK = 64
EXP_M05 = 0.6065306597126334
NEG_BIG = -1e30

VMEM_LIMIT = 56 * 1024 * 1024


def _sigmoid(x):
    return 1.0 / (1.0 + jnp.exp(-x))


def _split(a):
    hi = a.astype(BF16)
    lo = (a - hi.astype(F32)).astype(BF16)
    return hi, lo


def _mm(a, b, dims, passes):
    dn = (dims, ((), ()))
    if passes == 1:
        return lax.dot_general(a.astype(BF16), b.astype(BF16), dn, preferred_element_type=F32)
    ah, al = _split(a)
    bh, bl = _split(b)
    out = lax.dot_general(ah, bh, dn, preferred_element_type=F32)
    out += lax.dot_general(ah, bl, dn, preferred_element_type=F32)
    out += lax.dot_general(al, bh, dn, preferred_element_type=F32)
    return out


NN = ((1,), (0,))
NT = ((1,), (1,))
TN = ((0,), (0,))


def _mm_exact_rhs(a, b_bf16, passes=2):
    dn = (NN, ((), ()))
    out = None
    rem = a
    for _ in range(passes):
        part = rem.astype(BF16)
        term = lax.dot_general(part, b_bf16, dn, preferred_element_type=F32)
        out = term if out is None else out + term
        rem = rem - part.astype(F32)
    return out


def _mm_exact_lhs(a_bf16, b, passes=3):
    dn = (NN, ((), ()))
    out = None
    rem = b
    for _ in range(passes):
        part = rem.astype(BF16)
        term = lax.dot_general(a_bf16, part, dn, preferred_element_type=F32)
        out = term if out is None else out + term
        rem = rem - part.astype(F32)
    return out


def _ffn_kernel(x_ref, g_ref, wg_ref, wu_ref, wd_ref, *rest, nf, final):
    if final:
        fg_ref, o_ref, h_ref, acc_ref = rest
    else:
        o_ref, h_ref, acc_ref = rest
    j = pl.program_id(1)

    @pl.when(j == 0)
    def _():
        x = x_ref[...]
        ms = jnp.mean(x * x, axis=-1, keepdims=True)
        h_ref[...] = (x * lax.rsqrt(ms + RMS_EPS) * g_ref[...]).astype(BF16)
        acc_ref[...] = jnp.zeros_like(acc_ref)

    h = h_ref[...]
    g = jnp.dot(h, wg_ref[...], preferred_element_type=F32)
    u = jnp.dot(h, wu_ref[...], preferred_element_type=F32)
    act = (g * _sigmoid(g) * u).astype(BF16)
    acc_ref[...] += jnp.dot(act, wd_ref[...], preferred_element_type=F32)

    @pl.when(j == nf - 1)
    def _():
        y = x_ref[...] + 0.5 * acc_ref[...]
        if final:
            ms = jnp.mean(y * y, axis=-1, keepdims=True)
            y = y * lax.rsqrt(ms + RMS_EPS) * fg_ref[...]
        o_ref[...] = y


def _ffn(x, gain, wg, wu, wd, final_gain=None, *, tm=512, tf=1408):
    n, d = x.shape
    dff = wg.shape[1]
    nf = dff // tf
    final = final_gain is not None
    in_specs = [
        pl.BlockSpec((tm, d), lambda i, j: (i, 0)),
        pl.BlockSpec((1, d), lambda i, j: (0, 0)),
        pl.BlockSpec((d, tf), lambda i, j: (0, j)),
        pl.BlockSpec((d, tf), lambda i, j: (0, j)),
        pl.BlockSpec((tf, d), lambda i, j: (j, 0)),
    ]
    args = [x, gain.reshape(1, d), wg, wu, wd]
    if final:
        in_specs.append(pl.BlockSpec((1, d), lambda i, j: (0, 0)))
        args.append(final_gain.reshape(1, d))
    return pl.pallas_call(
        functools.partial(_ffn_kernel, nf=nf, final=final),
        grid=(n // tm, nf),
        in_specs=in_specs,
        out_specs=pl.BlockSpec((tm, d), lambda i, j: (i, 0)),
        out_shape=jax.ShapeDtypeStruct((n, d), F32),
        scratch_shapes=[pltpu.VMEM((tm, d), BF16), pltpu.VMEM((tm, d), F32)],
        compiler_params=pltpu.CompilerParams(
            dimension_semantics=("parallel", "arbitrary"), vmem_limit_bytes=VMEM_LIMIT),
        name="ffn_final" if final else "ffn",
    )(*args)


def _proj_kernel(x_ref, g_ref, w_ref, rkv_ref, lora_ref, att_ref, gate_ref, *, cols):
    x = x_ref[...]
    ms = jnp.mean(x * x, axis=-1, keepdims=True)
    h = (x * lax.rsqrt(ms + RMS_EPS) * g_ref[...]).astype(BF16)
    c0 = 0
    for ref, c in zip((rkv_ref, lora_ref, att_ref, gate_ref), cols):
        ref[...] = jnp.dot(h, w_ref[:, c0:c0 + c], preferred_element_type=F32)
        c0 += c


def _proj(x, gain, w, cols, *, tm=256):
    n, d = x.shape
    return pl.pallas_call(
        functools.partial(_proj_kernel, cols=cols),
        grid=(n // tm,),
        in_specs=[
            pl.BlockSpec((tm, d), lambda i: (i, 0)),
            pl.BlockSpec((1, d), lambda i: (0, 0)),
            pl.BlockSpec(w.shape, lambda i: (0, 0)),
        ],
        out_specs=[pl.BlockSpec((tm, c), lambda i: (i, 0)) for c in cols],
        out_shape=[jax.ShapeDtypeStruct((n, c), F32) for c in cols],
        compiler_params=pltpu.CompilerParams(
            dimension_semantics=("parallel",), vmem_limit_bytes=VMEM_LIMIT),
        name="proj",
    )(x, gain.reshape(1, d), w)


PAIR = 2 * HEAD_DIM
N_PAIRS = RWKV_HEADS // 2
RWKV_BLOCK = 256


def _dot(a, b, dims):
    return lax.dot_general(a, b, (dims, ((), ())), preferred_element_type=F32)


def _stack_heads(x, lane_lo):
    return jnp.concatenate([jnp.where(lane_lo, x, 0.0), jnp.where(lane_lo, 0.0, x)], axis=0).astype(BF16)


def _rwkv_kernel(rkv_ref, lora_ref, mu_rkv_ref, mu_lora_ref, wl_ref, w0_ref, a0_ref, kk_ref, ka_ref,
                 rk_ref, lnw_ref, lnb_ref, seg_ref, o_ref, s_ref, prev_rkv_ref, prev_lora_ref):
    t = pl.program_id(1)
    TB = rkv_ref.shape[1]
    L = CHUNK
    W = RWKV_WIDTH

    @pl.when(t == 0)
    def _():
        s_ref[...] = jnp.zeros_like(s_ref)
        prev_rkv_ref[...] = jnp.zeros_like(prev_rkv_ref)
        prev_lora_ref[...] = jnp.zeros_like(prev_lora_ref)

    row = lax.broadcasted_iota(jnp.int32, (TB, 1), 0)

    def lerp(p, prev_ref, mu):
        shifted = jnp.where(row == 0, prev_ref[...], pltpu.roll(p, 1, axis=0))
        prev_ref[...] = p[TB - 1:TB, :]
        return p + (shifted - p) * mu

    p = lerp(rkv_ref[0], prev_rkv_ref, mu_rkv_ref[...])
    pl_ = lerp(lora_ref[0], prev_lora_ref, mu_lora_ref[...])
    r = p[:, 0:W]
    k = p[:, W:2 * W]
    v = p[:, 2 * W:3 * W]

    lane = lax.broadcasted_iota(jnp.int32, (1, LORA_PAD), 1)
    act = jnp.where(lane < DECAY_LORA, jnp.tanh(pl_),
                    jnp.where(lane < DECAY_LORA + ICLR_LORA, pl_,
                              jnp.where(lane < DECAY_LORA + ICLR_LORA + GATE_LORA, _sigmoid(pl_), 0.0)))
    z = _mm(act, wl_ref[...], NN, 3)
    e = EXP_M05 * _sigmoid(z[:, 0:W] + w0_ref[...])
    a_ic = _sigmoid(z[:, W:2 * W] + a0_ref[...])
    gate = z[:, 2 * W:3 * W]

    seg = seg_ref[...]
    kk = k * kk_ref[...]
    kk = kk / jnp.maximum(jnp.sqrt(_mm_exact_rhs(kk * kk, seg)), 1e-12)
    k2 = k * (1.0 + (a_ic - 1.0) * ka_ref[...])
    bv = kk * a_ic

    ci = lax.broadcasted_iota(jnp.int32, (TB, TB), 0)
    cj = lax.broadcasted_iota(jnp.int32, (TB, TB), 1)
    tri = jnp.where((cj <= ci) & ((ci // L) == (cj // L)), 1.0, 0.0).astype(BF16)
    cw = -_mm_exact_lhs(tri, e, 3)
    at = -kk * jnp.exp(cw + e)
    rt = r * jnp.exp(cw)
    einv = jnp.exp(-cw)
    bt = bv * einv
    kt = k2 * einv

    ti = lax.broadcasted_iota(jnp.int32, (L, PAIR), 0)
    si = lax.broadcasted_iota(jnp.int32, (L, PAIR), 1) % HEAD_DIM
    strict = si < ti
    incl = si <= ti
    eye2 = jnp.where(si == ti, 1.0, 0.0)
    lane_lo = lax.broadcasted_iota(jnp.int32, (1, PAIR), 1) < HEAD_DIM
    bi = lax.broadcasted_iota(jnp.int32, (PAIR, PAIR), 0) // HEAD_DIM
    bj = lax.broadcasted_iota(jnp.int32, (PAIR, PAIR), 1) // HEAD_DIM
    same_head = bi == bj

    n_chunks = TB // L
    units = [(c, pr) for c in range(n_chunks) for pr in range(N_PAIRS)]
    rows = lambda c: slice(c * L, (c + 1) * L)
    lanes = lambda pr: slice(pr * PAIR, (pr + 1) * PAIR)
    w_end, eend = [], []
    for c in range(n_chunks):
        cw_c = cw[rows(c)]
        cw_last = cw_c[L - 1:L, :]
        eend.append(jnp.exp(cw_last - cw_c))
        w_end.append(jnp.exp(cw_last))

    a_st, v_st, ab, ak, rb, rk, bkw = {}, {}, {}, {}, {}, {}, {}
    for un in units:
        c, pr = un
        rc, lp = rows(c), lanes(pr)
        ar = jnp.concatenate([at[rc, lp], rt[rc, lp]], axis=0).astype(BF16)
        bk_s = jnp.concatenate([_stack_heads(bt[rc, lp], lane_lo),
                                _stack_heads(kt[rc, lp], lane_lo)], axis=0)
        sc = _dot(ar, bk_s, NT)
        ab[un] = jnp.where(strict, sc[0:L, 0:PAIR], 0.0)
        ak[un] = jnp.where(strict, sc[0:L, PAIR:2 * PAIR], 0.0)
        rb[un] = jnp.where(incl, sc[L:2 * L, 0:PAIR], 0.0)
        rk[un] = jnp.where(incl, sc[L:2 * L, PAIR:2 * PAIR], 0.0)
        a_st[un] = _stack_heads(at[rc, lp], lane_lo)
        v_st[un] = _stack_heads(v[rc, lp], lane_lo)
        bkw[un] = jnp.concatenate([bv[rc, lp] * eend[c][:, lp], k2[rc, lp] * eend[c][:, lp]],
                                  axis=0).astype(BF16)

    tm = {un: eye2 + ab[un] for un in units}
    pw = {un: _dot(ab[un].astype(BF16), _stack_heads(ab[un], lane_lo), NN) for un in units}
    for it in range(5):
        for un in units:
            pw_s = _stack_heads(pw[un], lane_lo)
            if it < 4:
                both = _dot(jnp.concatenate([tm[un], pw[un]], axis=0).astype(BF16), pw_s, NN)
                tm[un] = tm[un] + both[0:L]
                pw[un] = both[L:2 * L]
            else:
                tm[un] = tm[un] + _dot(tm[un].astype(BF16), pw_s, NN)

    m2 = {un: _dot(jnp.concatenate([ak[un], rk[un]], axis=0).astype(BF16), v_st[un], NN) for un in units}
    rp, y0, mp, cc = {}, {}, {}, {}
    tau = {un: _dot(tm[un].astype(BF16),
                    jnp.concatenate([a_st[un], _stack_heads(m2[un][0:L], lane_lo)], axis=1), NN)
           for un in units}
    for un in units:
        c, pr = un
        ap, u0 = tau[un][:, 0:PAIR], tau[un][:, PAIR:2 * PAIR]
        ry = _dot(rb[un].astype(BF16),
                  jnp.concatenate([_stack_heads(ap, lane_lo), _stack_heads(u0, lane_lo)], axis=1), NN)
        rp[un] = (rt[rows(c), lanes(pr)] + ry[:, 0:PAIR]).astype(BF16)
        y0[un] = ry[:, PAIR:2 * PAIR] + m2[un][L:2 * L]
    for un in units:
        c, pr = un
        ap, u0 = tau[un][:, 0:PAIR], tau[un][:, PAIR:2 * PAIR]
        mp[un] = jnp.where(same_head, _dot(ap.astype(BF16), bkw[un][0:L], TN), 0.0).astype(BF16)
        uv = jnp.concatenate([u0, v[rows(c), lanes(pr)]], axis=0).astype(BF16)
        cc[un] = jnp.where(same_head, _dot(uv, bkw[un], TN), 0.0)

    state = [s_ref[pr] for pr in range(N_PAIRS)]
    y_rows = []
    for c in range(n_chunks):
        y_cols = []
        for pr in range(N_PAIRS):
            un = (c, pr)
            s = state[pr]
            sb = s.astype(BF16)
            y_cols.append(_dot(rp[un], sb, NT) + y0[un])
            state[pr] = s * w_end[c][:, lanes(pr)] + _dot(sb, mp[un], NN) + cc[un]
        y_rows.append(jnp.concatenate(y_cols, axis=1))
    for pr in range(N_PAIRS):
        s_ref[pr] = state[pr]
    y = jnp.concatenate(y_rows, axis=0)

    inv_n = 1.0 / HEAD_DIM
    mean = _mm_exact_rhs(y, seg) * inv_n
    d = y - mean
    var = _mm_exact_rhs(d * d, seg) * inv_n
    yn = d * lax.rsqrt(var + GN_EPS) * lnw_ref[...] + lnb_ref[...]
    bonus = _mm_exact_rhs(r * k2 * rk_ref[...], seg) * v
    o_ref[0] = ((yn + bonus) * gate).astype(o_ref.dtype)


def _rwkv(p_rkv, p_lora, mu_rkv, mu_lora, w_lora, w0, a0, k_k, k_a, r_k, ln_w, ln_b, seg):
    b, t, _ = p_rkv.shape
    W = RWKV_WIDTH
    TB = RWKV_BLOCK
    row = lambda a: a.reshape(1, -1)
    const = lambda a: pl.BlockSpec(a.shape, lambda i, j: (0,) * a.ndim)
    consts = [row(mu_rkv), row(mu_lora), w_lora, row(w0), row(a0), row(k_k), row(k_a), row(r_k),
              row(ln_w), row(ln_b), seg]
    return pl.pallas_call(
        _rwkv_kernel,
        grid=(b, t // TB),
        in_specs=[pl.BlockSpec((1, TB, 3 * W), lambda i, j: (i, j, 0)),
                  pl.BlockSpec((1, TB, LORA_PAD), lambda i, j: (i, j, 0))] + [const(a) for a in consts],
        out_specs=pl.BlockSpec((1, TB, W), lambda i, j: (i, j, 0)),
        out_shape=jax.ShapeDtypeStruct((b, t, W), BF16),
        scratch_shapes=[pltpu.VMEM((N_PAIRS, PAIR, PAIR), F32),
                        pltpu.VMEM((1, 3 * W), F32), pltpu.VMEM((1, LORA_PAD), F32)],
        compiler_params=pltpu.CompilerParams(
            dimension_semantics=("parallel", "arbitrary"), vmem_limit_bytes=VMEM_LIMIT),
        name="rwkv",
    )(p_rkv, p_lora, *consts)


def _attn_kernel(sink_ref, q_ref, kvp_ref, kvc_ref, qn_ref, kn_ref, segq_ref, segk_ref, o_ref):
    blk = pl.program_id(1)
    T = q_ref.shape[1]
    inv_n = 1.0 / HEAD_DIM

    q = q_ref[0]
    q = q * lax.rsqrt(_mm_exact_rhs(q * q, segq_ref[...]) * inv_n + RMS_EPS) * qn_ref[...]

    def knorm(kv):
        kx = kv[:, 0:KV_WIDTH]
        return kx * lax.rsqrt(_mm_exact_rhs(kx * kx, segk_ref[...]) * inv_n + RMS_EPS) * kn_ref[...]

    kvp = kvp_ref[0]
    kvc = kvc_ref[0]
    kcat = jnp.concatenate([knorm(kvp), knorm(kvc)], axis=0).astype(BF16)
    vcat = jnp.concatenate([kvp[:, KV_WIDTH:], kvc[:, KV_WIDTH:]], axis=0).astype(BF16)

    qi = lax.broadcasted_iota(jnp.int32, (T, 2 * T), 0)
    kj = lax.broadcasted_iota(jnp.int32, (T, 2 * T), 1)
    mask = (kj <= qi + T) & (kj > qi + T - WINDOW) & ((blk > 0) | (kj >= T))
    scale = HEAD_DIM ** -0.5
    qb = q.astype(BF16)
    outs = []
    for h in range(ATT_Q_HEADS):
        g = h // ATT_GROUP
        qh = qb[:, h * HEAD_DIM:(h + 1) * HEAD_DIM]
        kg = kcat[:, g * HEAD_DIM:(g + 1) * HEAD_DIM]
        vg = vcat[:, g * HEAD_DIM:(g + 1) * HEAD_DIM]
        s = lax.dot_general(qh, kg, (NT, ((), ())), preferred_element_type=F32) * scale
        s = jnp.where(mask, s, NEG_BIG)
        sink = sink_ref[h]
        m = jnp.maximum(jnp.max(s, axis=-1, keepdims=True), sink)
        pe = jnp.exp(s - m)
        denom = jnp.sum(pe, axis=-1, keepdims=True) + jnp.exp(sink - m)
        probs = (pe / denom).astype(BF16)
        outs.append(jnp.dot(probs, vg, preferred_element_type=F32))
    o_ref[0] = jnp.concatenate(outs, axis=-1).astype(o_ref.dtype)


def _attn(p_att, q_norm, k_norm, sinks, segq, segk):
    b, t, _ = p_att.shape
    T = WINDOW
    nq = ATT_WIDTH // (2 * KV_WIDTH)
    const = lambda a: pl.BlockSpec(a.shape, lambda i, j: (0,) * a.ndim)
    qn = jnp.tile(q_norm, ATT_Q_HEADS).reshape(1, ATT_WIDTH)
    kn = jnp.tile(k_norm, ATT_KV_HEADS).reshape(1, KV_WIDTH)
    return pl.pallas_call(
        _attn_kernel,
        grid=(b, t // T),
        in_specs=[pl.BlockSpec(memory_space=pltpu.SMEM),
                  pl.BlockSpec((1, T, ATT_WIDTH), lambda i, j: (i, j, 0)),
                  pl.BlockSpec((1, T, 2 * KV_WIDTH), lambda i, j: (i, jnp.maximum(j - 1, 0), nq)),
                  pl.BlockSpec((1, T, 2 * KV_WIDTH), lambda i, j: (i, j, nq)),
                  const(qn), const(kn), const(segq), const(segk)],
        out_specs=pl.BlockSpec((1, T, ATT_WIDTH), lambda i, j: (i, j, 0)),
        out_shape=jax.ShapeDtypeStruct((b, t, ATT_WIDTH), BF16),
        compiler_params=pltpu.CompilerParams(
            dimension_semantics=("parallel", "parallel"), vmem_limit_bytes=VMEM_LIMIT),
        name="attn",
    )(sinks, p_att, p_att, p_att, qn, kn, segq, segk)


def _merge_kernel(x_ref, yr_ref, ya_ref, gate_ref, wr_ref, wa_ref, wo_ref, o_ref):
    d = x_ref.shape[1]
    gate = gate_ref[...]
    br = jnp.dot(yr_ref[...], wr_ref[...], preferred_element_type=F32)
    ba = jnp.dot(ya_ref[...], wa_ref[...], preferred_element_type=F32)
    merged = _sigmoid(gate[:, 0:d]) * br + _sigmoid(gate[:, d:2 * d]) * ba
    o_ref[...] = x_ref[...] + jnp.dot(merged.astype(BF16), wo_ref[...], preferred_element_type=F32)


def _merge(x, y_rwkv, y_att, p_gate, wr, wa, wo, *, tm=512):
    n, d = x.shape
    const = lambda a: pl.BlockSpec(a.shape, lambda i: (0,) * a.ndim)
    tile = lambda a: pl.BlockSpec((tm, a.shape[1]), lambda i: (i, 0))
    return pl.pallas_call(
        _merge_kernel,
        grid=(n // tm,),
        in_specs=[tile(x), tile(y_rwkv), tile(y_att), tile(p_gate), const(wr), const(wa), const(wo)],
        out_specs=tile(x),
        out_shape=jax.ShapeDtypeStruct((n, d), F32),
        compiler_params=pltpu.CompilerParams(
            dimension_semantics=("parallel",), vmem_limit_bytes=VMEM_LIMIT),
        name="merge",
    )(x, y_rwkv, y_att, p_gate, wr, wa, wo)


def _block_diag_ones(width):
    i = jnp.arange(width) // HEAD_DIM
    return (i[:, None] == i[None, :]).astype(BF16)


def _layer(x, ffn1_norm, ffn1_w_gate, ffn1_w_up, ffn1_w_down, mix_norm, w_in,
           rwkv_mu, rwkv_w0, rwkv_w_lora_up, rwkv_a0, rwkv_a_lora_up, rwkv_g_lora_up,
           rwkv_k_k, rwkv_k_a, rwkv_r_k, rwkv_ln_w, rwkv_ln_b,
           attn_q_norm, attn_k_norm, attn_sinks,
           w_branch_rwkv, w_branch_attn, w_out,
           ffn2_norm, ffn2_w_gate, ffn2_w_up, ffn2_w_down, final_norm):
    b, t, d = x.shape
    n = b * t
    W = RWKV_WIDTH
    n_lora = DECAY_LORA + ICLR_LORA + GATE_LORA
    rwkv_cols = 3 * W + n_lora
    att_cols = ATT_WIDTH + 2 * KV_WIDTH
    cols = (3 * W, LORA_PAD, att_cols, 2 * d)

    bf = lambda a: a.astype(BF16)
    x2 = x.reshape(n, d)
    x2 = _ffn(x2, ffn1_norm, bf(ffn1_w_gate), bf(ffn1_w_up), bf(ffn1_w_down))

    pad = jnp.zeros((d, LORA_PAD - n_lora), F32)
    w_in_p = bf(jnp.concatenate([w_in[:, :rwkv_cols], pad, w_in[:, rwkv_cols:]], axis=1))
    p_rkv, p_lora, p_att, p_gate = _proj(x2, mix_norm, w_in_p, cols)

    mu_rkv = rwkv_mu[:3 * W]
    mu_lora = jnp.concatenate([rwkv_mu[3 * W:], jnp.zeros((LORA_PAD - n_lora,), F32)])
    w_lora = jnp.zeros((LORA_PAD, 3 * W), F32)
    w_lora = w_lora.at[0:DECAY_LORA, 0:W].set(rwkv_w_lora_up)
    w_lora = w_lora.at[DECAY_LORA:DECAY_LORA + ICLR_LORA, W:2 * W].set(rwkv_a_lora_up)
    w_lora = w_lora.at[DECAY_LORA + ICLR_LORA:n_lora, 2 * W:3 * W].set(rwkv_g_lora_up)
    y_rwkv = _rwkv(p_rkv.reshape(b, t, 3 * W), p_lora.reshape(b, t, LORA_PAD), mu_rkv, mu_lora, w_lora,
                   rwkv_w0, rwkv_a0, rwkv_k_k, rwkv_k_a, rwkv_r_k.reshape(-1), rwkv_ln_w, rwkv_ln_b,
                   _block_diag_ones(W))
    y_att = _attn(p_att.reshape(b, t, att_cols), attn_q_norm, attn_k_norm, attn_sinks,
                  _block_diag_ones(ATT_WIDTH), _block_diag_ones(KV_WIDTH))

    x2 = _merge(x2, y_rwkv.reshape(n, W), y_att.reshape(n, ATT_WIDTH), p_gate,
                bf(w_branch_rwkv), bf(w_branch_attn), bf(w_out))
    x2 = _ffn(x2, ffn2_norm, bf(ffn2_w_gate), bf(ffn2_w_up), bf(ffn2_w_down), final_norm)
    return x2.reshape(b, t, d)


def kernel(x, ffn1_norm, ffn1_w_gate, ffn1_w_up, ffn1_w_down, mix_norm, w_in, rwkv_mu, rwkv_w0, rwkv_w_lora_up, rwkv_a0, rwkv_a_lora_up, rwkv_g_lora_up, rwkv_k_k, rwkv_k_a, rwkv_r_k, rwkv_ln_w, rwkv_ln_b, attn_q_norm, attn_k_norm, attn_sinks, w_branch_rwkv, w_branch_attn, w_out, ffn2_norm, ffn2_w_gate, ffn2_w_up, ffn2_w_down, final_norm):
    params = (ffn1_norm, ffn1_w_gate, ffn1_w_up, ffn1_w_down, mix_norm, w_in, rwkv_mu, rwkv_w0,
              rwkv_w_lora_up, rwkv_a0, rwkv_a_lora_up, rwkv_g_lora_up, rwkv_k_k, rwkv_k_a, rwkv_r_k,
              rwkv_ln_w, rwkv_ln_b, attn_q_norm, attn_k_norm, attn_sinks, w_branch_rwkv, w_branch_attn,
              w_out, ffn2_norm, ffn2_w_gate, ffn2_w_up, ffn2_w_down, final_norm)
    for layer in range(ffn1_norm.shape[0]):
        x = _layer(x, *(p[layer] for p in params))
    return x
```

```python
import functools

import jax
import jax.numpy as jnp
from jax import lax
from jax.experimental import pallas as pl
from jax.experimental.pallas import tpu as pltpu

F32 = jnp.float32
BF16 = jnp.bfloat16

HEAD_DIM = 64
RWKV_HEADS = 8
RWKV_WIDTH = RWKV_HEADS * HEAD_DIM
ATT_Q_HEADS = 8
ATT_KV_HEADS = 2
ATT_GROUP = ATT_Q_HEADS // ATT_KV_HEADS
ATT_WIDTH = ATT_Q_HEADS * HEAD_DIM
KV_WIDTH = ATT_KV_HEADS * HEAD_DIM
WINDOW = 128
DECAY_LORA = 32
ICLR_LORA = 32
GATE_LORA = 96
LORA_PAD = 256
RMS_EPS = 1e-6
GN_EPS = 64e-5
CHUNK = 64
EXP_M05 = 0.6065306597126334
NEG_BIG = -1e30

VMEM_LIMIT = 56 * 1024 * 1024


def _sigmoid(x):
    return 1.0 / (1.0 + jnp.exp(-x))


def _split(a):
    hi = a.astype(BF16)
    lo = (a - hi.astype(F32)).astype(BF16)
    return hi, lo


def _mm(a, b, dims, passes):
    dn = (dims, ((), ()))
    if passes == 1:
        return lax.dot_general(a.astype(BF16), b.astype(BF16), dn, preferred_element_type=F32)
    ah, al = _split(a)
    bh, bl = _split(b)
    out = lax.dot_general(ah, bh, dn, preferred_element_type=F32)
    out += lax.dot_general(ah, bl, dn, preferred_element_type=F32)
    out += lax.dot_general(al, bh, dn, preferred_element_type=F32)
    return out


NN = ((1,), (0,))
NT = ((1,), (1,))
TN = ((0,), (0,))


def _head_sums(x, seg):
    rows, cols = x.shape
    groups = cols // (2 * HEAD_DIM)
    hi = x.astype(BF16)
    lo = (x - hi.astype(F32)).astype(BF16)
    col = lambda a, g: a[:, g * 2 * HEAD_DIM:(g + 1) * 2 * HEAD_DIM]
    stacked = jnp.concatenate([col(hi, g) for g in range(groups)] + [col(lo, g) for g in range(groups)], axis=0)
    out = lax.dot_general(stacked, seg, (NN, ((), ())), preferred_element_type=F32)
    part = lambda i: out[i * rows:(i + 1) * rows]
    return jnp.concatenate([part(g) + part(groups + g) for g in range(groups)], axis=1)


def _mm_exact_lhs(a_bf16, b, passes=3):
    dn = (NN, ((), ()))
    out = None
    rem = b
    for _ in range(passes):
        part = rem.astype(BF16)
        term = lax.dot_general(a_bf16, part, dn, preferred_element_type=F32)
        out = term if out is None else out + term
        rem = rem - part.astype(F32)
    return out


def _ffn_kernel(x_ref, g_ref, wg_ref, wu_ref, wd_ref, *rest, nf, final):
    if final:
        fg_ref, o_ref, h_ref = rest
    else:
        o_ref, h_ref = rest
    j = pl.program_id(1)

    @pl.when(j == 0)
    def _():
        x = x_ref[...]
        ms = jnp.mean(x * x, axis=-1, keepdims=True)
        h_ref[...] = (x * lax.rsqrt(ms + RMS_EPS) * g_ref[...]).astype(BF16)
        o_ref[...] = x

    h = h_ref[...]
    g = jnp.dot(h, wg_ref[...], preferred_element_type=F32)
    u = jnp.dot(h, wu_ref[...], preferred_element_type=F32)
    act = (g * _sigmoid(g) * (0.5 * u)).astype(BF16)
    o_ref[...] += jnp.dot(act, wd_ref[...], preferred_element_type=F32)

    if final:
        @pl.when(j == nf - 1)
        def _():
            y = o_ref[...]
            ms = jnp.mean(y * y, axis=-1, keepdims=True)
            o_ref[...] = y * lax.rsqrt(ms + RMS_EPS) * fg_ref[...]


def _ffn(x, gain, wg, wu, wd, final_gain=None, *, tm=1024, tf=1408):
    n, d = x.shape
    dff = wg.shape[1]
    nf = dff // tf
    final = final_gain is not None
    in_specs = [
        pl.BlockSpec((tm, d), lambda i, j: (i, 0)),
        pl.BlockSpec((1, d), lambda i, j: (0, 0)),
        pl.BlockSpec((d, tf), lambda i, j: (0, j)),
        pl.BlockSpec((d, tf), lambda i, j: (0, j)),
        pl.BlockSpec((tf, d), lambda i, j: (j, 0)),
    ]
    args = [x, gain.reshape(1, d), wg, wu, wd]
    if final:
        in_specs.append(pl.BlockSpec((1, d), lambda i, j: (0, 0)))
        args.append(final_gain.reshape(1, d))
    return pl.pallas_call(
        functools.partial(_ffn_kernel, nf=nf, final=final),
        grid=(n // tm, nf),
        in_specs=in_specs,
        out_specs=pl.BlockSpec((tm, d), lambda i, j: (i, 0)),
        out_shape=jax.ShapeDtypeStruct((n, d), F32),
        scratch_shapes=[pltpu.VMEM((tm, d), BF16)],
        compiler_params=pltpu.CompilerParams(
            dimension_semantics=("parallel", "arbitrary"), vmem_limit_bytes=VMEM_LIMIT),
        name="ffn_final" if final else "ffn",
    )(*args)


def _proj_kernel(x_ref, g_ref, w_ref, rkv_ref, lora_ref, att_ref, gate_ref, *, cols):
    x = x_ref[...]
    ms = jnp.mean(x * x, axis=-1, keepdims=True)
    h = (x * lax.rsqrt(ms + RMS_EPS) * g_ref[...]).astype(BF16)
    c0 = 0
    for ref, c in zip((rkv_ref, lora_ref, att_ref, gate_ref), cols):
        ref[...] = jnp.dot(h, w_ref[:, c0:c0 + c], preferred_element_type=F32).astype(ref.dtype)
        c0 += c


PROJ_DTYPES = (F32, F32, F32, BF16)


def _proj(x, gain, w, cols, *, tm=256):
    n, d = x.shape
    return pl.pallas_call(
        functools.partial(_proj_kernel, cols=cols),
        grid=(n // tm,),
        in_specs=[
            pl.BlockSpec((tm, d), lambda i: (i, 0)),
            pl.BlockSpec((1, d), lambda i: (0, 0)),
            pl.BlockSpec(w.shape, lambda i: (0, 0)),
        ],
        out_specs=[pl.BlockSpec((tm, c), lambda i: (i, 0)) for c in cols],
        out_shape=[jax.ShapeDtypeStruct((n, c), dt) for c, dt in zip(cols, PROJ_DTYPES)],
        compiler_params=pltpu.CompilerParams(
            dimension_semantics=("parallel",), vmem_limit_bytes=VMEM_LIMIT),
        name="proj",
    )(x, gain.reshape(1, d), w)


PAIR = 2 * HEAD_DIM
N_PAIRS = RWKV_HEADS // 2
RWKV_BLOCK = 256
CUMSUM_PASSES = 2
LORA_PASSES = 1


def _dot(a, b, dims):
    return lax.dot_general(a, b, (dims, ((), ())), preferred_element_type=F32)


def _stack_heads(x, lane_lo):
    return jnp.concatenate([jnp.where(lane_lo, x, 0.0), jnp.where(lane_lo, 0.0, x)], axis=0).astype(BF16)


def _rwkv_kernel(rkv_ref, lora_ref, mu_rkv_ref, mu_lora_ref, wl_ref, w0_ref, a0_ref, kk_ref, ka_ref,
                 rk_ref, lnw_ref, lnb_ref, seg_ref, o_ref, s_ref, prev_rkv_ref, prev_lora_ref):
    t = pl.program_id(1)
    TB = rkv_ref.shape[1]
    L = CHUNK
    W = RWKV_WIDTH

    @pl.when(t == 0)
    def _():
        s_ref[...] = jnp.zeros_like(s_ref)
        prev_rkv_ref[...] = jnp.zeros_like(prev_rkv_ref)
        prev_lora_ref[...] = jnp.zeros_like(prev_lora_ref)

    row = lax.broadcasted_iota(jnp.int32, (TB, 1), 0)

    def lerp(p, prev_ref, mu):
        shifted = jnp.where(row == 0, prev_ref[...], pltpu.roll(p, 1, axis=0))
        prev_ref[...] = p[TB - 1:TB, :]
        return p + (shifted - p) * mu

    p = lerp(rkv_ref[0], prev_rkv_ref, mu_rkv_ref[...])
    pl_ = lerp(lora_ref[0], prev_lora_ref, mu_lora_ref[...])
    r = p[:, 0:W]
    k = p[:, W:2 * W]
    v = p[:, 2 * W:3 * W]

    lane = lax.broadcasted_iota(jnp.int32, (1, LORA_PAD), 1)
    act = jnp.where(lane < DECAY_LORA, jnp.tanh(pl_),
                    jnp.where(lane < DECAY_LORA + ICLR_LORA, pl_,
                              jnp.where(lane < DECAY_LORA + ICLR_LORA + GATE_LORA, _sigmoid(pl_), 0.0)))
    z = _mm(act, wl_ref[...], NN, LORA_PASSES)
    e = EXP_M05 * _sigmoid(z[:, 0:W] + w0_ref[...])
    a_ic = _sigmoid(z[:, W:2 * W] + a0_ref[...])
    gate = z[:, 2 * W:3 * W]

    seg = seg_ref[...]
    kk = k * kk_ref[...]
    kk = kk / jnp.maximum(jnp.sqrt(_head_sums(kk * kk, seg)), 1e-12)
    k2 = k * (1.0 + (a_ic - 1.0) * ka_ref[...])
    bv = kk * a_ic

    ci = lax.broadcasted_iota(jnp.int32, (TB, TB), 0)
    cj = lax.broadcasted_iota(jnp.int32, (TB, TB), 1)
    tri = jnp.where((cj <= ci) & ((ci // L) == (cj // L)), 1.0, 0.0).astype(BF16)
    cw = -_mm_exact_lhs(tri, e, CUMSUM_PASSES)
    at = -kk * jnp.exp(cw + e)
    rt = r * jnp.exp(cw)
    einv = jnp.exp(-cw)
    bt = bv * einv
    kt = k2 * einv

    ti = lax.broadcasted_iota(jnp.int32, (L, PAIR), 0)
    si = lax.broadcasted_iota(jnp.int32, (L, PAIR), 1) % HEAD_DIM
    strict = si < ti
    incl = si <= ti
    eye2 = jnp.where(si == ti, 1.0, 0.0)
    lane_lo = lax.broadcasted_iota(jnp.int32, (1, PAIR), 1) < HEAD_DIM
    bi = lax.broadcasted_iota(jnp.int32, (PAIR, PAIR), 0) // HEAD_DIM
    bj = lax.broadcasted_iota(jnp.int32, (PAIR, PAIR), 1) // HEAD_DIM
    same_head = bi == bj

    n_chunks = TB // L
    units = [(c, pr) for c in range(n_chunks) for pr in range(N_PAIRS)]
    rows = lambda c: slice(c * L, (c + 1) * L)
    lanes = lambda pr: slice(pr * PAIR, (pr + 1) * PAIR)
    w_end, eend = [], []
    for c in range(n_chunks):
        cw_c = cw[rows(c)]
        cw_last = cw_c[L - 1:L, :]
        eend.append(jnp.exp(cw_last - cw_c))
        w_end.append(jnp.exp(cw_last))

    a_st, v_st, ab, ak, rb, rk, bkw = {}, {}, {}, {}, {}, {}, {}
    for un in units:
        c, pr = un
        rc, lp = rows(c), lanes(pr)
        ar = jnp.concatenate([at[rc, lp], rt[rc, lp]], axis=0).astype(BF16)
        bk_s = jnp.concatenate([_stack_heads(bt[rc, lp], lane_lo),
                                _stack_heads(kt[rc, lp], lane_lo)], axis=0)
        sc = _dot(ar, bk_s, NT)
        ab[un] = jnp.where(strict, sc[0:L, 0:PAIR], 0.0)
        ak[un] = jnp.where(strict, sc[0:L, PAIR:2 * PAIR], 0.0)
        rb[un] = jnp.where(incl, sc[L:2 * L, 0:PAIR], 0.0)
        rk[un] = jnp.where(incl, sc[L:2 * L, PAIR:2 * PAIR], 0.0)
        a_st[un] = _stack_heads(at[rc, lp], lane_lo)
        v_st[un] = _stack_heads(v[rc, lp], lane_lo)
        bkw[un] = jnp.concatenate([bv[rc, lp] * eend[c][:, lp], k2[rc, lp] * eend[c][:, lp]],
                                  axis=0).astype(BF16)

    tm = {un: eye2 + ab[un] for un in units}
    pw = {un: _dot(ab[un].astype(BF16), _stack_heads(ab[un], lane_lo), NN) for un in units}
    for it in range(5):
        for un in units:
            pw_s = _stack_heads(pw[un], lane_lo)
            if it < 4:
                both = _dot(jnp.concatenate([tm[un], pw[un]], axis=0).astype(BF16), pw_s, NN)
                tm[un] = tm[un] + both[0:L]
                pw[un] = both[L:2 * L]
            else:
                tm[un] = tm[un] + _dot(tm[un].astype(BF16), pw_s, NN)

    m2 = {un: _dot(jnp.concatenate([ak[un], rk[un]], axis=0).astype(BF16), v_st[un], NN) for un in units}
    rp, y0, mp, cc = {}, {}, {}, {}
    tau = {un: _dot(tm[un].astype(BF16),
                    jnp.concatenate([a_st[un], _stack_heads(m2[un][0:L], lane_lo)], axis=1), NN)
           for un in units}
    for un in units:
        c, pr = un
        ap, u0 = tau[un][:, 0:PAIR], tau[un][:, PAIR:2 * PAIR]
        ry = _dot(rb[un].astype(BF16),
                  jnp.concatenate([_stack_heads(ap, lane_lo), _stack_heads(u0, lane_lo)], axis=1), NN)
        rp[un] = (rt[rows(c), lanes(pr)] + ry[:, 0:PAIR]).astype(BF16)
        y0[un] = ry[:, PAIR:2 * PAIR] + m2[un][L:2 * L]
    for un in units:
        c, pr = un
        ap, u0 = tau[un][:, 0:PAIR], tau[un][:, PAIR:2 * PAIR]
        mp[un] = jnp.where(same_head, _dot(ap.astype(BF16), bkw[un][0:L], TN), 0.0).astype(BF16)
        uv = jnp.concatenate([u0, v[rows(c), lanes(pr)]], axis=0).astype(BF16)
        cc[un] = jnp.where(same_head, _dot(uv, bkw[un], TN), 0.0)

    state = [s_ref[pr] for pr in range(N_PAIRS)]
    y_rows = []
    for c in range(n_chunks):
        y_cols = []
        for pr in range(N_PAIRS):
            un = (c, pr)
            s = state[pr]
            sb = s.astype(BF16)
            y_cols.append(_dot(rp[un], sb, NT) + y0[un])
            state[pr] = s * w_end[c][:, lanes(pr)] + _dot(sb, mp[un], NN) + cc[un]
        y_rows.append(jnp.concatenate(y_cols, axis=1))
    for pr in range(N_PAIRS):
        s_ref[pr] = state[pr]
    y = jnp.concatenate(y_rows, axis=0)

    inv_n = 1.0 / HEAD_DIM
    mean = _head_sums(y, seg) * inv_n
    d = y - mean
    var = _head_sums(d * d, seg) * inv_n
    yn = d * lax.rsqrt(var + GN_EPS) * lnw_ref[...] + lnb_ref[...]
    bonus = _head_sums(r * k2 * rk_ref[...], seg) * v
    o_ref[0] = ((yn + bonus) * gate).astype(o_ref.dtype)


def _rwkv(p_rkv, p_lora, mu_rkv, mu_lora, w_lora, w0, a0, k_k, k_a, r_k, ln_w, ln_b, seg):
    b, t, _ = p_rkv.shape
    W = RWKV_WIDTH
    TB = RWKV_BLOCK
    row = lambda a: a.reshape(1, -1)
    const = lambda a: pl.BlockSpec(a.shape, lambda i, j: (0,) * a.ndim)
    consts = [row(mu_rkv), row(mu_lora), w_lora, row(w0), row(a0), row(k_k), row(k_a), row(r_k),
              row(ln_w), row(ln_b), seg]
    return pl.pallas_call(
        _rwkv_kernel,
        grid=(b, t // TB),
        in_specs=[pl.BlockSpec((1, TB, 3 * W), lambda i, j: (i, j, 0)),
                  pl.BlockSpec((1, TB, LORA_PAD), lambda i, j: (i, j, 0))] + [const(a) for a in consts],
        out_specs=pl.BlockSpec((1, TB, W), lambda i, j: (i, j, 0)),
        out_shape=jax.ShapeDtypeStruct((b, t, W), BF16),
        scratch_shapes=[pltpu.VMEM((N_PAIRS, PAIR, PAIR), F32),
                        pltpu.VMEM((1, 3 * W), F32), pltpu.VMEM((1, LORA_PAD), F32)],
        compiler_params=pltpu.CompilerParams(
            dimension_semantics=("parallel", "arbitrary"), vmem_limit_bytes=VMEM_LIMIT),
        name="rwkv",
    )(p_rkv, p_lora, *consts)


ATT_QBLOCK = 512
LOG2E = 1.4426950408889634


def _attn_kernel(sink_ref, q_ref, kvp_ref, kvc_ref, qn_ref, kn_ref, seg_ref, o_ref):
    blk = pl.program_id(1)
    QB = q_ref.shape[1]
    T = WINDOW
    inv_n = 1.0 / HEAD_DIM
    seg = seg_ref[...]

    q = q_ref[0]
    q = q * lax.rsqrt(_head_sums(q * q, seg) * inv_n + RMS_EPS) * qn_ref[...]
    kv = jnp.concatenate([kvp_ref[0], kvc_ref[0]], axis=0)
    k = kv[:, 0:KV_WIDTH]
    v = kv[:, KV_WIDTH:2 * KV_WIDTH]
    k = k * lax.rsqrt(_head_sums(k * k, seg) * inv_n + RMS_EPS) * kn_ref[...]

    lane_lo = lax.broadcasted_iota(jnp.int32, (1, KV_WIDTH), 1) < HEAD_DIM
    k_sw = pltpu.roll(k, HEAD_DIM, axis=1)
    v_sw = pltpu.roll(v, HEAD_DIM, axis=1)
    kd = (jnp.where(lane_lo, k, k_sw).astype(BF16), jnp.where(lane_lo, k_sw, k).astype(BF16))
    vd = (jnp.where(lane_lo, v, v_sw).astype(BF16), jnp.where(lane_lo, v_sw, v).astype(BF16))

    qi = lax.broadcasted_iota(jnp.int32, (T, 2 * T), 0)
    kj = lax.broadcasted_iota(jnp.int32, (T, 2 * T), 1)
    band = (kj <= qi + T) & (kj > qi + T - WINDOW)
    first = band & ((blk > 0) | (kj >= T))

    for i in range(QB // T):
        rq = slice(i * T, (i + 1) * T)
        rk = slice(i * T, (i + 2) * T)
        mask = first if i == 0 else band
        scores = []
        for g in range(ATT_KV_HEADS):
            parts = []
            for pr in (2 * g, 2 * g + 1):
                q_p = q[rq, pr * KV_WIDTH:(pr + 1) * KV_WIDTH]
                parts += [jnp.where(lane_lo, q_p, 0.0), jnp.where(lane_lo, 0.0, q_p)]
            lhs = jnp.concatenate(parts, axis=0).astype(BF16)
            scores.append(_dot(lhs, kd[g][rk], NT))
        outs = []
        for g in range(ATT_KV_HEADS):
            probs = []
            for hh in range(ATT_GROUP):
                s = jnp.where(mask, scores[g][hh * T:(hh + 1) * T], NEG_BIG)
                sink = sink_ref[g * ATT_GROUP + hh]
                m = jnp.maximum(jnp.max(s, axis=-1, keepdims=True), sink)
                pe = jnp.exp2(s - m)
                denom = jnp.sum(pe, axis=-1, keepdims=True) + jnp.exp2(sink - m)
                probs.append((pe * (1.0 / denom)).astype(BF16))
            o = _dot(jnp.concatenate(probs, axis=0), vd[g][rk], NN)
            outs += [jnp.where(lane_lo, o[0:T], o[T:2 * T]), jnp.where(lane_lo, o[2 * T:3 * T], o[3 * T:4 * T])]
        o_ref[0, rq, :] = jnp.concatenate(outs, axis=1).astype(o_ref.dtype)


def _attn(p_att, q_norm, k_norm, sinks, seg):
    b, t, _ = p_att.shape
    T = WINDOW
    QB = ATT_QBLOCK
    nq = ATT_WIDTH // (2 * KV_WIDTH)
    const = lambda a: pl.BlockSpec(a.shape, lambda i, j: (0,) * a.ndim)
    qn = (jnp.tile(q_norm, ATT_Q_HEADS) * (HEAD_DIM ** -0.5 * LOG2E)).reshape(1, ATT_WIDTH)
    kn = jnp.tile(k_norm, ATT_KV_HEADS).reshape(1, KV_WIDTH)
    return pl.pallas_call(
        _attn_kernel,
        grid=(b, t // QB),
        in_specs=[pl.BlockSpec(memory_space=pltpu.SMEM),
                  pl.BlockSpec((1, QB, ATT_WIDTH), lambda i, j: (i, j, 0)),
                  pl.BlockSpec((1, T, 2 * KV_WIDTH), lambda i, j: (i, jnp.maximum(j * (QB // T) - 1, 0), nq)),
                  pl.BlockSpec((1, QB, 2 * KV_WIDTH), lambda i, j: (i, j, nq)),
                  const(qn), const(kn), const(seg)],
        out_specs=pl.BlockSpec((1, QB, ATT_WIDTH), lambda i, j: (i, j, 0)),
        out_shape=jax.ShapeDtypeStruct((b, t, ATT_WIDTH), BF16),
        compiler_params=pltpu.CompilerParams(
            dimension_semantics=("parallel", "parallel"), vmem_limit_bytes=VMEM_LIMIT),
        name="attn",
    )(sinks * LOG2E, p_att, p_att, p_att, qn, kn, seg)


def _merge_kernel(x_ref, yr_ref, ya_ref, gate_ref, wr_ref, wa_ref, wo_ref, o_ref):
    d = x_ref.shape[1]
    gate = gate_ref[...].astype(F32)
    br = jnp.dot(yr_ref[...], wr_ref[...], preferred_element_type=F32)
    ba = jnp.dot(ya_ref[...], wa_ref[...], preferred_element_type=F32)
    merged = _sigmoid(gate[:, 0:d]) * br + _sigmoid(gate[:, d:2 * d]) * ba
    o_ref[...] = x_ref[...] + jnp.dot(merged.astype(BF16), wo_ref[...], preferred_element_type=F32)


def _merge(x, y_rwkv, y_att, p_gate, wr, wa, wo, *, tm=512):
    n, d = x.shape
    const = lambda a: pl.BlockSpec(a.shape, lambda i: (0,) * a.ndim)
    tile = lambda a: pl.BlockSpec((tm, a.shape[1]), lambda i: (i, 0))
    return pl.pallas_call(
        _merge_kernel,
        grid=(n // tm,),
        in_specs=[tile(x), tile(y_rwkv), tile(y_att), tile(p_gate), const(wr), const(wa), const(wo)],
        out_specs=tile(x),
        out_shape=jax.ShapeDtypeStruct((n, d), F32),
        compiler_params=pltpu.CompilerParams(
            dimension_semantics=("parallel",), vmem_limit_bytes=VMEM_LIMIT),
        name="merge",
    )(x, y_rwkv, y_att, p_gate, wr, wa, wo)


def _pair_ones():
    i = jnp.arange(2 * HEAD_DIM) // HEAD_DIM
    return (i[:, None] == i[None, :]).astype(BF16)


def _layer(x, ffn1_norm, ffn1_w_gate, ffn1_w_up, ffn1_w_down, mix_norm, w_in,
           rwkv_mu, rwkv_w0, rwkv_w_lora_up, rwkv_a0, rwkv_a_lora_up, rwkv_g_lora_up,
           rwkv_k_k, rwkv_k_a, rwkv_r_k, rwkv_ln_w, rwkv_ln_b,
           attn_q_norm, attn_k_norm, attn_sinks,
           w_branch_rwkv, w_branch_attn, w_out,
           ffn2_norm, ffn2_w_gate, ffn2_w_up, ffn2_w_down, final_norm):
    b, t, d = x.shape
    n = b * t
    W = RWKV_WIDTH
    n_lora = DECAY_LORA + ICLR_LORA + GATE_LORA
    rwkv_cols = 3 * W + n_lora
    att_cols = ATT_WIDTH + 2 * KV_WIDTH
    cols = (3 * W, LORA_PAD, att_cols, 2 * d)

    bf = lambda a: a.astype(BF16)
    x2 = x.reshape(n, d)
    x2 = _ffn(x2, ffn1_norm, bf(ffn1_w_gate), bf(ffn1_w_up), bf(ffn1_w_down))

    pad = jnp.zeros((d, LORA_PAD - n_lora), F32)
    w_in_p = bf(jnp.concatenate([w_in[:, :rwkv_cols], pad, w_in[:, rwkv_cols:]], axis=1))
    p_rkv, p_lora, p_att, p_gate = _proj(x2, mix_norm, w_in_p, cols)

    mu_rkv = rwkv_mu[:3 * W]
    mu_lora = jnp.concatenate([rwkv_mu[3 * W:], jnp.zeros((LORA_PAD - n_lora,), F32)])
    w_lora = jnp.zeros((LORA_PAD, 3 * W), F32)
    w_lora = w_lora.at[0:DECAY_LORA, 0:W].set(rwkv_w_lora_up)
    w_lora = w_lora.at[DECAY_LORA:DECAY_LORA + ICLR_LORA, W:2 * W].set(rwkv_a_lora_up)
    w_lora = w_lora.at[DECAY_LORA + ICLR_LORA:n_lora, 2 * W:3 * W].set(rwkv_g_lora_up)
    seg = _pair_ones()
    y_rwkv = _rwkv(p_rkv.reshape(b, t, 3 * W), p_lora.reshape(b, t, LORA_PAD), mu_rkv, mu_lora, w_lora,
                   rwkv_w0, rwkv_a0, rwkv_k_k, rwkv_k_a, rwkv_r_k.reshape(-1), rwkv_ln_w, rwkv_ln_b,
                   seg)
    y_att = _attn(p_att.reshape(b, t, att_cols), attn_q_norm, attn_k_norm, attn_sinks,
                  seg)

    x2 = _merge(x2, y_rwkv.reshape(n, W), y_att.reshape(n, ATT_WIDTH), p_gate,
                bf(w_branch_rwkv), bf(w_branch_attn), bf(w_out))
    x2 = _ffn(x2, ffn2_norm, bf(ffn2_w_gate), bf(ffn2_w_up), bf(ffn2_w_down), final_norm)
    return x2.reshape(b, t, d)


def kernel(x, ffn1_norm, ffn1_w_gate, ffn1_w_up, ffn1_w_down, mix_norm, w_in, rwkv_mu, rwkv_w0, rwkv_w_lora_up, rwkv_a0, rwkv_a_lora_up, rwkv_g_lora_up, rwkv_k_k, rwkv_k_a, rwkv_r_k, rwkv_ln_w, rwkv_ln_b, attn_q_norm, attn_k_norm, attn_sinks, w_branch_rwkv, w_branch_attn, w_out, ffn2_norm, ffn2_w_gate, ffn2_w_up, ffn2_w_down, final_norm):
    params = (ffn1_norm, ffn1_w_gate, ffn1_w_up, ffn1_w_down, mix_norm, w_in, rwkv_mu, rwkv_w0,
              rwkv_w_lora_up, rwkv_a0, rwkv_a_lora_up, rwkv_g_lora_up, rwkv_k_k, rwkv_k_a, rwkv_r_k,
              rwkv_ln_w, rwkv_ln_b, attn_q_norm, attn_k_norm, attn_sinks, w_branch_rwkv, w_branch_attn,
              w_out, ffn2_norm, ffn2_w_gate, ffn2_w_up, ffn2_w_down, final_norm)
    for layer in range(ffn1_norm.shape[0]):
        x = _layer(x, *(p[layer] for p in params))
    return x
```

```python
import functools

import jax
import jax.numpy as jnp
from jax import lax
from jax.experimental import pallas as pl
from jax.experimental.pallas import tpu as pltpu

F32 = jnp.float32
BF16 = jnp.bfloat16

HEAD_DIM = 64
RWKV_HEADS = 8
RWKV_WIDTH = RWKV_HEADS * HEAD_DIM
ATT_Q_HEADS = 8
ATT_KV_HEADS = 2
ATT_GROUP = ATT_Q_HEADS // ATT_KV_HEADS
ATT_WIDTH = ATT_Q_HEADS * HEAD_DIM
KV_WIDTH = ATT_KV_HEADS * HEAD_DIM
WINDOW = 128
DECAY_LORA = 32
ICLR_LORA = 32
GATE_LORA = 96
LORA_PAD = 256
RMS_EPS = 1e-6
GN_EPS = 64e-5
CHUNK = 64
EXP_M05 = 0.6065306597126334
NEG_BIG = -1e30

MXU_TILE = 256
VMEM_LIMIT = 56 * 1024 * 1024


def _sigmoid(x):
    return 1.0 / (1.0 + jnp.exp(-x))


def _split(a):
    hi = a.astype(BF16)
    lo = (a - hi.astype(F32)).astype(BF16)
    return hi, lo


def _mm(a, b, dims, passes):
    dn = (dims, ((), ()))
    if passes == 1:
        return lax.dot_general(a.astype(BF16), b.astype(BF16), dn, preferred_element_type=F32)
    ah, al = _split(a)
    bh, bl = _split(b)
    out = lax.dot_general(ah, bh, dn, preferred_element_type=F32)
    out += lax.dot_general(ah, bl, dn, preferred_element_type=F32)
    out += lax.dot_general(al, bh, dn, preferred_element_type=F32)
    return out


NN = ((1,), (0,))
NT = ((1,), (1,))
TN = ((0,), (0,))


HEAD_SUM_TERMS = 1


def _head_sums(x, seg, terms=HEAD_SUM_TERMS):
    rows, cols = x.shape
    width = min(cols, seg.shape[0])
    groups = cols // width
    parts, rem = [], x
    for i in range(terms):
        parts.append(rem.astype(BF16))
        if i + 1 < terms:
            rem = rem - parts[-1].astype(F32)
    stacked = jnp.concatenate([p[:, g * width:(g + 1) * width] for p in parts for g in range(groups)], axis=0)
    out = lax.dot_general(stacked, seg[0:width, 0:width], (NN, ((), ())), preferred_element_type=F32)
    blk = lambda i: out[i * rows:(i + 1) * rows]
    sums = []
    for g in range(groups):
        acc = blk(g)
        for i in range(1, terms):
            acc = acc + blk(i * groups + g)
        sums.append(acc)
    return jnp.concatenate(sums, axis=1)


def _mm_exact_lhs(a_bf16, b, passes=3):
    dn = (NN, ((), ()))
    out = None
    rem = b
    for _ in range(passes):
        part = rem.astype(BF16)
        term = lax.dot_general(a_bf16, part, dn, preferred_element_type=F32)
        out = term if out is None else out + term
        rem = rem - part.astype(F32)
    return out


def _ffn_kernel(x_ref, g_ref, wg_ref, wu_ref, wd_ref, *rest, final):
    if final:
        fg_ref, o_ref, h_ref, act_ref = rest
    else:
        o_ref, h_ref, act_ref = rest
    x = x_ref[...]
    ms = jnp.mean(x * x, axis=-1, keepdims=True)
    h_ref[...] = (x * lax.rsqrt(ms + RMS_EPS) * g_ref[...]).astype(BF16)
    dff = wg_ref.shape[1]
    for c in range(dff // MXU_TILE):
        sl = slice(c * MXU_TILE, (c + 1) * MXU_TILE)
        g = jnp.dot(h_ref[...], wg_ref[:, sl], preferred_element_type=F32)
        u = jnp.dot(h_ref[...], wu_ref[:, sl], preferred_element_type=F32)
        act_ref[:, sl] = (g * _sigmoid(g) * (0.5 * u)).astype(BF16)
    y = x_ref[...] + jnp.dot(act_ref[...], wd_ref[...], preferred_element_type=F32)
    if final:
        ms = jnp.mean(y * y, axis=-1, keepdims=True)
        y = y * lax.rsqrt(ms + RMS_EPS) * fg_ref[...]
    o_ref[...] = y


def _ffn(x, gain, wg, wu, wd, final_gain=None, *, tm=1024):
    n, d = x.shape
    dff = wg.shape[1]
    final = final_gain is not None
    resident = lambda a: pl.BlockSpec(a.shape, lambda i: (0,) * a.ndim, pipeline_mode=pl.Buffered(1))
    in_specs = [
        pl.BlockSpec((tm, d), lambda i: (i, 0)),
        pl.BlockSpec((1, d), lambda i: (0, 0)),
        resident(wg), resident(wu), resident(wd),
    ]
    args = [x, gain.reshape(1, d), wg, wu, wd]
    if final:
        in_specs.append(pl.BlockSpec((1, d), lambda i: (0, 0)))
        args.append(final_gain.reshape(1, d))
    return pl.pallas_call(
        functools.partial(_ffn_kernel, final=final),
        grid=(n // tm,),
        in_specs=in_specs,
        out_specs=pl.BlockSpec((tm, d), lambda i: (i, 0)),
        out_shape=jax.ShapeDtypeStruct((n, d), F32),
        scratch_shapes=[pltpu.VMEM((tm, d), BF16), pltpu.VMEM((tm, dff), BF16)],
        compiler_params=pltpu.CompilerParams(
            dimension_semantics=("parallel",), vmem_limit_bytes=VMEM_LIMIT),
        name="ffn_final" if final else "ffn",
    )(*args)


def _proj_kernel(x_ref, g_ref, w_ref, rkv_ref, lora_ref, att_ref, gate_ref, *, cols):
    x = x_ref[...]
    ms = jnp.mean(x * x, axis=-1, keepdims=True)
    h = (x * lax.rsqrt(ms + RMS_EPS) * g_ref[...]).astype(BF16)
    c0 = 0
    for ref, c in zip((rkv_ref, lora_ref, att_ref, gate_ref), cols):
        ref[...] = jnp.dot(h, w_ref[:, c0:c0 + c], preferred_element_type=F32).astype(ref.dtype)
        c0 += c


PROJ_DTYPES = (F32, F32, F32, BF16)


def _proj(x, gain, w, cols, *, tm=512):
    n, d = x.shape
    return pl.pallas_call(
        functools.partial(_proj_kernel, cols=cols),
        grid=(n // tm,),
        in_specs=[
            pl.BlockSpec((tm, d), lambda i: (i, 0)),
            pl.BlockSpec((1, d), lambda i: (0, 0)),
            pl.BlockSpec(w.shape, lambda i: (0, 0), pipeline_mode=pl.Buffered(1)),
        ],
        out_specs=[pl.BlockSpec((tm, c), lambda i: (i, 0)) for c in cols],
        out_shape=[jax.ShapeDtypeStruct((n, c), dt) for c, dt in zip(cols, PROJ_DTYPES)],
        compiler_params=pltpu.CompilerParams(
            dimension_semantics=("parallel",), vmem_limit_bytes=VMEM_LIMIT),
        name="proj",
    )(x, gain.reshape(1, d), w)


PAIR = 2 * HEAD_DIM
N_PAIRS = RWKV_HEADS // 2
RWKV_BLOCK = 512
CUMSUM_PASSES = 2
LORA_PASSES = 1


def _dot(a, b, dims):
    return lax.dot_general(a, b, (dims, ((), ())), preferred_element_type=F32)


def _stack_heads(x, lane_lo):
    return jnp.concatenate([jnp.where(lane_lo, x, 0.0), jnp.where(lane_lo, 0.0, x)], axis=0).astype(BF16)


def _rwkv_kernel(rkv_ref, lora_ref, mu_rkv_ref, mu_lora_ref, wl_ref, w0_ref, a0_ref, kk_ref, ka_ref,
                 rk_ref, lnw_ref, lnb_ref, seg_ref, o_ref, s_ref, prev_rkv_ref, prev_lora_ref):
    t = pl.program_id(1)
    TB = rkv_ref.shape[1]
    L = CHUNK
    W = RWKV_WIDTH

    @pl.when(t == 0)
    def _():
        s_ref[...] = jnp.zeros_like(s_ref)
        prev_rkv_ref[...] = jnp.zeros_like(prev_rkv_ref)
        prev_lora_ref[...] = jnp.zeros_like(prev_lora_ref)

    row = lax.broadcasted_iota(jnp.int32, (TB, 1), 0)

    def lerp(p, prev_ref, mu):
        shifted = jnp.where(row == 0, prev_ref[...], pltpu.roll(p, 1, axis=0))
        prev_ref[...] = p[TB - 1:TB, :]
        return p + (shifted - p) * mu

    p = lerp(rkv_ref[0], prev_rkv_ref, mu_rkv_ref[...])
    pl_ = lerp(lora_ref[0], prev_lora_ref, mu_lora_ref[...])
    r = p[:, 0:W]
    k = p[:, W:2 * W]
    v = p[:, 2 * W:3 * W]

    lane = lax.broadcasted_iota(jnp.int32, (1, LORA_PAD), 1)
    act = jnp.where(lane < DECAY_LORA, jnp.tanh(pl_),
                    jnp.where(lane < DECAY_LORA + ICLR_LORA, pl_,
                              jnp.where(lane < DECAY_LORA + ICLR_LORA + GATE_LORA, _sigmoid(pl_), 0.0)))
    z = _mm(act, wl_ref[...], NN, LORA_PASSES)
    e = EXP_M05 * _sigmoid(z[:, 0:W] + w0_ref[...])
    a_ic = _sigmoid(z[:, W:2 * W] + a0_ref[...])
    gate = z[:, 2 * W:3 * W]

    seg = seg_ref[...]
    kk = k * kk_ref[...]
    kk = kk / jnp.maximum(jnp.sqrt(_head_sums(kk * kk, seg)), 1e-12)
    k2 = k * (1.0 + (a_ic - 1.0) * ka_ref[...])
    bv = kk * a_ic

    ci = lax.broadcasted_iota(jnp.int32, (MXU_TILE, MXU_TILE), 0)
    cj = lax.broadcasted_iota(jnp.int32, (MXU_TILE, MXU_TILE), 1)
    tri = jnp.where((cj <= ci) & ((ci // L) == (cj // L)), 1.0, 0.0).astype(BF16)
    cw = -jnp.concatenate([_mm_exact_lhs(tri, e[i:i + MXU_TILE], CUMSUM_PASSES)
                           for i in range(0, TB, MXU_TILE)], axis=0)
    at = -kk * jnp.exp(cw + e)
    rt = r * jnp.exp(cw)
    einv = jnp.exp(-cw)
    bt = bv * einv
    kt = k2 * einv

    ti = lax.broadcasted_iota(jnp.int32, (L, PAIR), 0)
    si = lax.broadcasted_iota(jnp.int32, (L, PAIR), 1) % HEAD_DIM
    strict = si < ti
    incl = si <= ti
    eye2 = jnp.where(si == ti, 1.0, 0.0)
    lane_lo = lax.broadcasted_iota(jnp.int32, (1, PAIR), 1) < HEAD_DIM
    bi = lax.broadcasted_iota(jnp.int32, (PAIR, PAIR), 0) // HEAD_DIM
    bj = lax.broadcasted_iota(jnp.int32, (PAIR, PAIR), 1) // HEAD_DIM
    same_head = bi == bj

    n_chunks = TB // L
    units = [(c, pr) for c in range(n_chunks) for pr in range(N_PAIRS)]
    rows = lambda c: slice(c * L, (c + 1) * L)
    lanes = lambda pr: slice(pr * PAIR, (pr + 1) * PAIR)
    w_end, eend = [], []
    for c in range(n_chunks):
        cw_c = cw[rows(c)]
        cw_last = cw_c[L - 1:L, :]
        eend.append(jnp.exp(cw_last - cw_c))
        w_end.append(jnp.exp(cw_last))

    a_st, v_st, ab, ak, rb, rk, bkw = {}, {}, {}, {}, {}, {}, {}
    for un in units:
        c, pr = un
        rc, lp = rows(c), lanes(pr)
        ar = jnp.concatenate([at[rc, lp], rt[rc, lp]], axis=0).astype(BF16)
        bk_s = jnp.concatenate([_stack_heads(bt[rc, lp], lane_lo),
                                _stack_heads(kt[rc, lp], lane_lo)], axis=0)
        sc = _dot(ar, bk_s, NT)
        ab[un] = jnp.where(strict, sc[0:L, 0:PAIR], 0.0)
        ak[un] = jnp.where(strict, sc[0:L, PAIR:2 * PAIR], 0.0)
        rb[un] = jnp.where(incl, sc[L:2 * L, 0:PAIR], 0.0)
        rk[un] = jnp.where(incl, sc[L:2 * L, PAIR:2 * PAIR], 0.0)
        a_st[un] = _stack_heads(at[rc, lp], lane_lo)
        v_st[un] = _stack_heads(v[rc, lp], lane_lo)
        bkw[un] = jnp.concatenate([bv[rc, lp] * eend[c][:, lp], k2[rc, lp] * eend[c][:, lp]],
                                  axis=0).astype(BF16)

    tm = {un: eye2 + ab[un] for un in units}
    pw = {un: _dot(ab[un].astype(BF16), _stack_heads(ab[un], lane_lo), NN) for un in units}
    for it in range(5):
        for un in units:
            pw_s = _stack_heads(pw[un], lane_lo)
            if it < 4:
                both = _dot(jnp.concatenate([tm[un], pw[un]], axis=0).astype(BF16), pw_s, NN)
                tm[un] = tm[un] + both[0:L]
                pw[un] = both[L:2 * L]
            else:
                tm[un] = tm[un] + _dot(tm[un].astype(BF16), pw_s, NN)

    m2 = {un: _dot(jnp.concatenate([ak[un], rk[un]], axis=0).astype(BF16), v_st[un], NN) for un in units}
    rp, y0, mp, cc = {}, {}, {}, {}
    tau = {un: _dot(tm[un].astype(BF16),
                    jnp.concatenate([a_st[un], _stack_heads(m2[un][0:L], lane_lo)], axis=1), NN)
           for un in units}
    for un in units:
        c, pr = un
        ap, u0 = tau[un][:, 0:PAIR], tau[un][:, PAIR:2 * PAIR]
        ry = _dot(rb[un].astype(BF16),
                  jnp.concatenate([_stack_heads(ap, lane_lo), _stack_heads(u0, lane_lo)], axis=1), NN)
        rp[un] = (rt[rows(c), lanes(pr)] + ry[:, 0:PAIR]).astype(BF16)
        y0[un] = ry[:, PAIR:2 * PAIR] + m2[un][L:2 * L]
    for un in units:
        c, pr = un
        ap, u0 = tau[un][:, 0:PAIR], tau[un][:, PAIR:2 * PAIR]
        mp[un] = jnp.where(same_head, _dot(ap.astype(BF16), bkw[un][0:L], TN), 0.0).astype(BF16)
        uv = jnp.concatenate([u0, v[rows(c), lanes(pr)]], axis=0).astype(BF16)
        cc[un] = jnp.where(same_head, _dot(uv, bkw[un], TN), 0.0)

    state = [s_ref[pr] for pr in range(N_PAIRS)]
    y_rows = []
    for c in range(n_chunks):
        y_cols = []
        for pr in range(N_PAIRS):
            un = (c, pr)
            s = state[pr]
            sb = s.astype(BF16)
            y_cols.append(_dot(rp[un], sb, NT) + y0[un])
            state[pr] = s * w_end[c][:, lanes(pr)] + _dot(sb, mp[un], NN) + cc[un]
        y_rows.append(jnp.concatenate(y_cols, axis=1))
    for pr in range(N_PAIRS):
        s_ref[pr] = state[pr]
    y = jnp.concatenate(y_rows, axis=0)

    inv_n = 1.0 / HEAD_DIM
    mean = _head_sums(y, seg) * inv_n
    d = y - mean
    var = _head_sums(d * d, seg) * inv_n
    yn = d * lax.rsqrt(var + GN_EPS) * lnw_ref[...] + lnb_ref[...]
    bonus = _head_sums(r * k2 * rk_ref[...], seg) * v
    o_ref[0] = ((yn + bonus) * gate).astype(o_ref.dtype)


def _rwkv(p_rkv, p_lora, mu_rkv, mu_lora, w_lora, w0, a0, k_k, k_a, r_k, ln_w, ln_b, seg):
    b, t, _ = p_rkv.shape
    W = RWKV_WIDTH
    TB = RWKV_BLOCK
    row = lambda a: a.reshape(1, -1)
    const = lambda a: pl.BlockSpec(a.shape, lambda i, j: (0,) * a.ndim)
    consts = [row(mu_rkv), row(mu_lora), w_lora, row(w0), row(a0), row(k_k), row(k_a), row(r_k),
              row(ln_w), row(ln_b), seg]
    return pl.pallas_call(
        _rwkv_kernel,
        grid=(b, t // TB),
        in_specs=[pl.BlockSpec((1, TB, 3 * W), lambda i, j: (i, j, 0)),
                  pl.BlockSpec((1, TB, LORA_PAD), lambda i, j: (i, j, 0))] + [const(a) for a in consts],
        out_specs=pl.BlockSpec((1, TB, W), lambda i, j: (i, j, 0)),
        out_shape=jax.ShapeDtypeStruct((b, t, W), BF16),
        scratch_shapes=[pltpu.VMEM((N_PAIRS, PAIR, PAIR), F32),
                        pltpu.VMEM((1, 3 * W), F32), pltpu.VMEM((1, LORA_PAD), F32)],
        compiler_params=pltpu.CompilerParams(
            dimension_semantics=("parallel", "arbitrary"), vmem_limit_bytes=VMEM_LIMIT),
        name="rwkv",
    )(p_rkv, p_lora, *consts)


ATT_QBLOCK = 512
LOG2E = 1.4426950408889634


def _attn_kernel(sink_ref, q_ref, kvp_ref, kvc_ref, qn_ref, kn_ref, seg_ref, o_ref):
    blk = pl.program_id(1)
    QB = q_ref.shape[1]
    T = WINDOW
    inv_n = 1.0 / HEAD_DIM
    seg = seg_ref[...]

    q = q_ref[0]
    q = q * lax.rsqrt(_head_sums(q * q, seg) * inv_n + RMS_EPS) * qn_ref[...]
    kv = jnp.concatenate([kvp_ref[0], kvc_ref[0]], axis=0)
    k = kv[:, 0:KV_WIDTH]
    v = kv[:, KV_WIDTH:2 * KV_WIDTH]
    k = k * lax.rsqrt(_head_sums(k * k, seg) * inv_n + RMS_EPS) * kn_ref[...]

    lane_lo = lax.broadcasted_iota(jnp.int32, (1, KV_WIDTH), 1) < HEAD_DIM
    k_sw = pltpu.roll(k, HEAD_DIM, axis=1)
    v_sw = pltpu.roll(v, HEAD_DIM, axis=1)
    kd = (jnp.where(lane_lo, k, k_sw).astype(BF16), jnp.where(lane_lo, k_sw, k).astype(BF16))
    vd = (jnp.where(lane_lo, v, v_sw).astype(BF16), jnp.where(lane_lo, v_sw, v).astype(BF16))

    qi = lax.broadcasted_iota(jnp.int32, (T, 2 * T), 0)
    kj = lax.broadcasted_iota(jnp.int32, (T, 2 * T), 1)
    band = (kj <= qi + T) & (kj > qi + T - WINDOW)
    first = band & ((blk > 0) | (kj >= T))

    for i in range(QB // T):
        rq = slice(i * T, (i + 1) * T)
        rk = slice(i * T, (i + 2) * T)
        mask = first if i == 0 else band
        scores = []
        for g in range(ATT_KV_HEADS):
            parts = []
            for pr in (2 * g, 2 * g + 1):
                q_p = q[rq, pr * KV_WIDTH:(pr + 1) * KV_WIDTH]
                parts += [jnp.where(lane_lo, q_p, 0.0), jnp.where(lane_lo, 0.0, q_p)]
            lhs = jnp.concatenate(parts, axis=0).astype(BF16)
            scores.append(_dot(lhs, kd[g][rk], NT))
        outs = []
        for g in range(ATT_KV_HEADS):
            probs = []
            for hh in range(ATT_GROUP):
                s = jnp.where(mask, scores[g][hh * T:(hh + 1) * T], NEG_BIG)
                sink = sink_ref[g * ATT_GROUP + hh]
                m = jnp.maximum(jnp.max(s, axis=-1, keepdims=True), sink)
                pe = jnp.exp2(s - m)
                denom = jnp.sum(pe, axis=-1, keepdims=True) + jnp.exp2(sink - m)
                probs.append((pe * (1.0 / denom)).astype(BF16))
            o = _dot(jnp.concatenate(probs, axis=0), vd[g][rk], NN)
            outs += [jnp.where(lane_lo, o[0:T], o[T:2 * T]), jnp.where(lane_lo, o[2 * T:3 * T], o[3 * T:4 * T])]
        o_ref[0, rq, :] = jnp.concatenate(outs, axis=1).astype(o_ref.dtype)


def _attn(p_att, q_norm, k_norm, sinks, seg):
    b, t, _ = p_att.shape
    T = WINDOW
    QB = ATT_QBLOCK
    nq = ATT_WIDTH // (2 * KV_WIDTH)
    const = lambda a: pl.BlockSpec(a.shape, lambda i, j: (0,) * a.ndim)
    qn = (jnp.tile(q_norm, ATT_Q_HEADS) * (HEAD_DIM ** -0.5 * LOG2E)).reshape(1, ATT_WIDTH)
    kn = jnp.tile(k_norm, ATT_KV_HEADS).reshape(1, KV_WIDTH)
    return pl.pallas_call(
        _attn_kernel,
        grid=(b, t // QB),
        in_specs=[pl.BlockSpec(memory_space=pltpu.SMEM),
                  pl.BlockSpec((1, QB, ATT_WIDTH), lambda i, j: (i, j, 0)),
                  pl.BlockSpec((1, T, 2 * KV_WIDTH), lambda i, j: (i, jnp.maximum(j * (QB // T) - 1, 0), nq)),
                  pl.BlockSpec((1, QB, 2 * KV_WIDTH), lambda i, j: (i, j, nq)),
                  const(qn), const(kn), const(seg)],
        out_specs=pl.BlockSpec((1, QB, ATT_WIDTH), lambda i, j: (i, j, 0)),
        out_shape=jax.ShapeDtypeStruct((b, t, ATT_WIDTH), BF16),
        compiler_params=pltpu.CompilerParams(
            dimension_semantics=("parallel", "parallel"), vmem_limit_bytes=VMEM_LIMIT),
        name="attn",
    )(sinks * LOG2E, p_att, p_att, p_att, qn, kn, seg)


def _merge_kernel(x_ref, yr_ref, ya_ref, gate_ref, wr_ref, wa_ref, wo_ref, o_ref):
    d = x_ref.shape[1]
    gate = gate_ref[...].astype(F32)
    br = jnp.dot(yr_ref[...], wr_ref[...], preferred_element_type=F32)
    ba = jnp.dot(ya_ref[...], wa_ref[...], preferred_element_type=F32)
    merged = _sigmoid(gate[:, 0:d]) * br + _sigmoid(gate[:, d:2 * d]) * ba
    o_ref[...] = x_ref[...] + jnp.dot(merged.astype(BF16), wo_ref[...], preferred_element_type=F32)


def _merge(x, y_rwkv, y_att, p_gate, wr, wa, wo, *, tm=512):
    n, d = x.shape
    const = lambda a: pl.BlockSpec(a.shape, lambda i: (0,) * a.ndim)
    tile = lambda a: pl.BlockSpec((tm, a.shape[1]), lambda i: (i, 0))
    return pl.pallas_call(
        _merge_kernel,
        grid=(n // tm,),
        in_specs=[tile(x), tile(y_rwkv), tile(y_att), tile(p_gate), const(wr), const(wa), const(wo)],
        out_specs=tile(x),
        out_shape=jax.ShapeDtypeStruct((n, d), F32),
        compiler_params=pltpu.CompilerParams(
            dimension_semantics=("parallel",), vmem_limit_bytes=VMEM_LIMIT),
        name="merge",
    )(x, y_rwkv, y_att, p_gate, wr, wa, wo)


def _head_ones():
    i = jnp.arange(MXU_TILE) // HEAD_DIM
    return (i[:, None] == i[None, :]).astype(BF16)


def _layer(x, ffn1_norm, ffn1_w_gate, ffn1_w_up, ffn1_w_down, mix_norm, w_in,
           rwkv_mu, rwkv_w0, rwkv_w_lora_up, rwkv_a0, rwkv_a_lora_up, rwkv_g_lora_up,
           rwkv_k_k, rwkv_k_a, rwkv_r_k, rwkv_ln_w, rwkv_ln_b,
           attn_q_norm, attn_k_norm, attn_sinks,
           w_branch_rwkv, w_branch_attn, w_out,
           ffn2_norm, ffn2_w_gate, ffn2_w_up, ffn2_w_down, final_norm):
    b, t, d = x.shape
    n = b * t
    W = RWKV_WIDTH
    n_lora = DECAY_LORA + ICLR_LORA + GATE_LORA
    rwkv_cols = 3 * W + n_lora
    att_cols = ATT_WIDTH + 2 * KV_WIDTH
    cols = (3 * W, LORA_PAD, att_cols, 2 * d)

    bf = lambda a: a.astype(BF16)
    x2 = x.reshape(n, d)
    x2 = _ffn(x2, ffn1_norm, bf(ffn1_w_gate), bf(ffn1_w_up), bf(ffn1_w_down))

    pad = jnp.zeros((d, LORA_PAD - n_lora), F32)
    w_in_p = bf(jnp.concatenate([w_in[:, :rwkv_cols], pad, w_in[:, rwkv_cols:]], axis=1))
    p_rkv, p_lora, p_att, p_gate = _proj(x2, mix_norm, w_in_p, cols)

    mu_rkv = rwkv_mu[:3 * W]
    mu_lora = jnp.concatenate([rwkv_mu[3 * W:], jnp.zeros((LORA_PAD - n_lora,), F32)])
    w_lora = jnp.zeros((LORA_PAD, 3 * W), F32)
    w_lora = w_lora.at[0:DECAY_LORA, 0:W].set(rwkv_w_lora_up)
    w_lora = w_lora.at[DECAY_LORA:DECAY_LORA + ICLR_LORA, W:2 * W].set(rwkv_a_lora_up)
    w_lora = w_lora.at[DECAY_LORA + ICLR_LORA:n_lora, 2 * W:3 * W].set(rwkv_g_lora_up)
    seg = _head_ones()
    y_rwkv = _rwkv(p_rkv.reshape(b, t, 3 * W), p_lora.reshape(b, t, LORA_PAD), mu_rkv, mu_lora, w_lora,
                   rwkv_w0, rwkv_a0, rwkv_k_k, rwkv_k_a, rwkv_r_k.reshape(-1), rwkv_ln_w, rwkv_ln_b,
                   seg)
    y_att = _attn(p_att.reshape(b, t, att_cols), attn_q_norm, attn_k_norm, attn_sinks,
                  seg)

    x2 = _merge(x2, y_rwkv.reshape(n, W), y_att.reshape(n, ATT_WIDTH), p_gate,
                bf(w_branch_rwkv), bf(w_branch_attn), bf(w_out))
    x2 = _ffn(x2, ffn2_norm, bf(ffn2_w_gate), bf(ffn2_w_up), bf(ffn2_w_down), final_norm)
    return x2.reshape(b, t, d)


def kernel(x, ffn1_norm, ffn1_w_gate, ffn1_w_up, ffn1_w_down, mix_norm, w_in, rwkv_mu, rwkv_w0, rwkv_w_lora_up, rwkv_a0, rwkv_a_lora_up, rwkv_g_lora_up, rwkv_k_k, rwkv_k_a, rwkv_r_k, rwkv_ln_w, rwkv_ln_b, attn_q_norm, attn_k_norm, attn_sinks, w_branch_rwkv, w_branch_attn, w_out, ffn2_norm, ffn2_w_gate, ffn2_w_up, ffn2_w_down, final_norm):
    params = (ffn1_norm, ffn1_w_gate, ffn1_w_up, ffn1_w_down, mix_norm, w_in, rwkv_mu, rwkv_w0,
              rwkv_w_lora_up, rwkv_a0, rwkv_a_lora_up, rwkv_g_lora_up, rwkv_k_k, rwkv_k_a, rwkv_r_k,
              rwkv_ln_w, rwkv_ln_b, attn_q_norm, attn_k_norm, attn_sinks, w_branch_rwkv, w_branch_attn,
              w_out, ffn2_norm, ffn2_w_gate, ffn2_w_up, ffn2_w_down, final_norm)
    for layer in range(ffn1_norm.shape[0]):
        x = _layer(x, *(p[layer] for p in params))
    return x
```

```python
import functools

import jax
import jax.numpy as jnp
from jax import lax
from jax.experimental import pallas as pl
from jax.experimental.pallas import tpu as pltpu

F32 = jnp.float32
BF16 = jnp.bfloat16

HEAD_DIM = 64
RWKV_HEADS = 8
RWKV_WIDTH = RWKV_HEADS * HEAD_DIM
ATT_Q_HEADS = 8
ATT_KV_HEADS = 2
ATT_GROUP = ATT_Q_HEADS // ATT_KV_HEADS
ATT_WIDTH = ATT_Q_HEADS * HEAD_DIM
KV_WIDTH = ATT_KV_HEADS * HEAD_DIM
WINDOW = 128
DECAY_LORA = 32
ICLR_LORA = 32
GATE_LORA = 96
LORA_PAD = 256
RMS_EPS = 1e-6
GN_EPS = 64e-5
CHUNK = 64
EXP_M05 = 0.6065306597126334
NEG_BIG = -1e30

MXU_TILE = 256
VMEM_LIMIT = 56 * 1024 * 1024
FFN_TILE = 1024
MERGE_FFN_TILE = 1024


def _sigmoid(x):
    return 0.5 * jnp.tanh(0.5 * x) + 0.5


def _split(a):
    hi = a.astype(BF16)
    lo = (a - hi.astype(F32)).astype(BF16)
    return hi, lo


def _mm(a, b, dims, passes):
    dn = (dims, ((), ()))
    if passes == 1:
        return lax.dot_general(a.astype(BF16), b.astype(BF16), dn, preferred_element_type=F32)
    ah, al = _split(a)
    bh, bl = _split(b)
    out = lax.dot_general(ah, bh, dn, preferred_element_type=F32)
    out += lax.dot_general(ah, bl, dn, preferred_element_type=F32)
    out += lax.dot_general(al, bh, dn, preferred_element_type=F32)
    return out


NN = ((1,), (0,))
NT = ((1,), (1,))
TN = ((0,), (0,))


HEAD_SUM_TERMS = 1


def _head_sums(x, seg, terms=HEAD_SUM_TERMS):
    rows, cols = x.shape
    width = min(cols, seg.shape[0])
    groups = cols // width
    parts, rem = [], x
    for i in range(terms):
        parts.append(rem.astype(BF16))
        if i + 1 < terms:
            rem = rem - parts[-1].astype(F32)
    stacked = jnp.concatenate([p[:, g * width:(g + 1) * width] for p in parts for g in range(groups)], axis=0)
    out = lax.dot_general(stacked, seg[0:width, 0:width], (NN, ((), ())), preferred_element_type=F32)
    blk = lambda i: out[i * rows:(i + 1) * rows]
    sums = []
    for g in range(groups):
        acc = blk(g)
        for i in range(1, terms):
            acc = acc + blk(i * groups + g)
        sums.append(acc)
    return jnp.concatenate(sums, axis=1)


def _mm_exact_lhs(a_bf16, b, passes=3):
    dn = (NN, ((), ()))
    out = None
    rem = b
    for _ in range(passes):
        part = rem.astype(BF16)
        term = lax.dot_general(a_bf16, part, dn, preferred_element_type=F32)
        out = term if out is None else out + term
        rem = rem - part.astype(F32)
    return out


def _ffn_kernel(*refs, final, merge):
    refs = list(refs)
    take = lambda k: [refs.pop(0) for _ in range(k)]
    (x_ref,) = take(1)
    if merge:
        yr_ref, ya_ref, gate_ref, wr_ref, wa_ref, wo_ref = take(6)
    g_ref, wg_ref, wu_ref, wd_ref = take(4)
    if final:
        (fg_ref,) = take(1)
    o_ref, h_ref, act_ref = refs

    x = x_ref[...]
    if merge:
        d = x.shape[1]
        gate = gate_ref[...].astype(F32)
        br = jnp.dot(yr_ref[...], wr_ref[...], preferred_element_type=F32)
        ba = jnp.dot(ya_ref[...], wa_ref[...], preferred_element_type=F32)
        merged = _sigmoid(gate[:, 0:d]) * br + _sigmoid(gate[:, d:2 * d]) * ba
        x = x + jnp.dot(merged.astype(BF16), wo_ref[...], preferred_element_type=F32)
        o_ref[...] = x
    ms = jnp.mean(x * x, axis=-1, keepdims=True)
    h_ref[...] = (x * lax.rsqrt(ms + RMS_EPS) * g_ref[...]).astype(BF16)
    dff = wg_ref.shape[1]
    for c in range(dff // MXU_TILE):
        sl = slice(c * MXU_TILE, (c + 1) * MXU_TILE)
        g = jnp.dot(h_ref[...], wg_ref[:, sl], preferred_element_type=F32)
        u = jnp.dot(h_ref[...], wu_ref[:, sl], preferred_element_type=F32)
        act_ref[:, sl] = (g * _sigmoid(g) * (0.5 * u)).astype(BF16)
    resid = o_ref[...] if merge else x_ref[...]
    y = resid + jnp.dot(act_ref[...], wd_ref[...], preferred_element_type=F32)
    if final:
        ms = jnp.mean(y * y, axis=-1, keepdims=True)
        y = y * lax.rsqrt(ms + RMS_EPS) * fg_ref[...]
    o_ref[...] = y


def _ffn(x, gain, wg, wu, wd, final_gain=None, merge_args=None, *, tm):
    n, d = x.shape
    dff = wg.shape[1]
    final = final_gain is not None
    merge = merge_args is not None
    tile = lambda a: pl.BlockSpec((tm, a.shape[1]), lambda i: (i, 0))
    row = lambda a: pl.BlockSpec((1, a.shape[-1]), lambda i: (0, 0))
    resident = lambda a: pl.BlockSpec(a.shape, lambda i: (0,) * a.ndim, pipeline_mode=pl.Buffered(1))
    in_specs, args = [tile(x)], [x]
    if merge:
        in_specs += [tile(a) for a in merge_args[:3]] + [resident(a) for a in merge_args[3:]]
        args += list(merge_args)
    in_specs += [row(gain), resident(wg), resident(wu), resident(wd)]
    args += [gain.reshape(1, d), wg, wu, wd]
    if final:
        in_specs.append(row(final_gain))
        args.append(final_gain.reshape(1, d))
    return pl.pallas_call(
        functools.partial(_ffn_kernel, final=final, merge=merge),
        grid=(n // tm,),
        in_specs=in_specs,
        out_specs=tile(x),
        out_shape=jax.ShapeDtypeStruct((n, d), F32),
        scratch_shapes=[pltpu.VMEM((tm, d), BF16), pltpu.VMEM((tm, dff), BF16)],
        compiler_params=pltpu.CompilerParams(
            dimension_semantics=("parallel",), vmem_limit_bytes=VMEM_LIMIT),
        name="merge_ffn" if merge else "ffn",
    )(*args)


def _proj_kernel(x_ref, g_ref, *refs):
    w_refs, out_refs = refs[:len(refs) // 2], refs[len(refs) // 2:]
    x = x_ref[...]
    ms = jnp.mean(x * x, axis=-1, keepdims=True)
    h = (x * lax.rsqrt(ms + RMS_EPS) * g_ref[...]).astype(BF16)
    for w_ref, o_ref in zip(w_refs, out_refs):
        o_ref[...] = jnp.dot(h, w_ref[...], preferred_element_type=F32).astype(o_ref.dtype)


PROJ_DTYPES = (F32, F32, F32, BF16)


def _proj(x, gain, weights, *, tm=512):
    n, d = x.shape
    resident = lambda a: pl.BlockSpec(a.shape, lambda i: (0, 0), pipeline_mode=pl.Buffered(1))
    return pl.pallas_call(
        _proj_kernel,
        grid=(n // tm,),
        in_specs=[pl.BlockSpec((tm, d), lambda i: (i, 0)), pl.BlockSpec((1, d), lambda i: (0, 0))]
        + [resident(w) for w in weights],
        out_specs=[pl.BlockSpec((tm, w.shape[1]), lambda i: (i, 0)) for w in weights],
        out_shape=[jax.ShapeDtypeStruct((n, w.shape[1]), dt) for w, dt in zip(weights, PROJ_DTYPES)],
        compiler_params=pltpu.CompilerParams(
            dimension_semantics=("parallel",), vmem_limit_bytes=VMEM_LIMIT),
        name="proj",
    )(x, gain.reshape(1, d), *weights)


PAIR = 2 * HEAD_DIM
N_PAIRS = RWKV_HEADS // 2
RWKV_BLOCK = 512
CUMSUM_PASSES = 2
LORA_PASSES = 1


def _dot(a, b, dims):
    return lax.dot_general(a, b, (dims, ((), ())), preferred_element_type=F32)


def _stack_heads(x, lane_lo):
    return jnp.concatenate([jnp.where(lane_lo, x, 0.0), jnp.where(lane_lo, 0.0, x)], axis=0).astype(BF16)


def _rwkv_kernel(rkv_ref, lora_ref, mu_rkv_ref, mu_lora_ref, wl_ref, w0_ref, a0_ref, kk_ref, ka_ref,
                 rk_ref, lnw_ref, lnb_ref, seg_ref, o_ref, s_ref, prev_rkv_ref, prev_lora_ref):
    t = pl.program_id(1)
    TB = rkv_ref.shape[1]
    L = CHUNK
    W = RWKV_WIDTH

    @pl.when(t == 0)
    def _():
        s_ref[...] = jnp.zeros_like(s_ref)
        prev_rkv_ref[...] = jnp.zeros_like(prev_rkv_ref)
        prev_lora_ref[...] = jnp.zeros_like(prev_lora_ref)

    row = lax.broadcasted_iota(jnp.int32, (TB, 1), 0)

    def lerp(p, prev_ref, mu):
        shifted = jnp.where(row == 0, prev_ref[...], pltpu.roll(p, 1, axis=0))
        prev_ref[...] = p[TB - 1:TB, :]
        return p + (shifted - p) * mu

    p = lerp(rkv_ref[0], prev_rkv_ref, mu_rkv_ref[...])
    pl_ = lerp(lora_ref[0], prev_lora_ref, mu_lora_ref[...])
    r = p[:, 0:W]
    k = p[:, W:2 * W]
    v = p[:, 2 * W:3 * W]

    lane = lax.broadcasted_iota(jnp.int32, (1, LORA_PAD), 1)
    act = jnp.where(lane < DECAY_LORA, jnp.tanh(pl_),
                    jnp.where(lane < DECAY_LORA + ICLR_LORA, pl_,
                              jnp.where(lane < DECAY_LORA + ICLR_LORA + GATE_LORA, _sigmoid(pl_), 0.0)))
    z = _mm(act, wl_ref[...], NN, LORA_PASSES)
    e = EXP_M05 * _sigmoid(z[:, 0:W] + w0_ref[...])
    a_ic = _sigmoid(z[:, W:2 * W] + a0_ref[...])
    gate = z[:, 2 * W:3 * W]

    seg = seg_ref[...]
    kk = k * kk_ref[...]
    kk = kk * lax.rsqrt(jnp.maximum(_head_sums(kk * kk, seg), 1e-24))
    k2 = k * (1.0 + (a_ic - 1.0) * ka_ref[...])
    bv = kk * a_ic

    ci = lax.broadcasted_iota(jnp.int32, (MXU_TILE, MXU_TILE), 0)
    cj = lax.broadcasted_iota(jnp.int32, (MXU_TILE, MXU_TILE), 1)
    tri = jnp.where((cj <= ci) & ((ci // L) == (cj // L)), 1.0, 0.0).astype(BF16)
    cw = -jnp.concatenate([_mm_exact_lhs(tri, e[i:i + MXU_TILE], CUMSUM_PASSES)
                           for i in range(0, TB, MXU_TILE)], axis=0)
    at = -kk * jnp.exp(cw + e)
    rt = r * jnp.exp(cw)
    einv = jnp.exp(-cw)
    bt = bv * einv
    kt = k2 * einv

    ti = lax.broadcasted_iota(jnp.int32, (L, PAIR), 0)
    si = lax.broadcasted_iota(jnp.int32, (L, PAIR), 1) % HEAD_DIM
    strict = si < ti
    incl = si <= ti
    eye2 = jnp.where(si == ti, 1.0, 0.0)
    lane_lo = lax.broadcasted_iota(jnp.int32, (1, PAIR), 1) < HEAD_DIM
    bi = lax.broadcasted_iota(jnp.int32, (PAIR, PAIR), 0) // HEAD_DIM
    bj = lax.broadcasted_iota(jnp.int32, (PAIR, PAIR), 1) // HEAD_DIM
    same_head = bi == bj

    n_chunks = TB // L
    units = [(c, pr) for c in range(n_chunks) for pr in range(N_PAIRS)]
    rows = lambda c: slice(c * L, (c + 1) * L)
    lanes = lambda pr: slice(pr * PAIR, (pr + 1) * PAIR)
    w_end, eend = [], []
    for c in range(n_chunks):
        cw_c = cw[rows(c)]
        cw_last = cw_c[L - 1:L, :]
        eend.append(jnp.exp(cw_last - cw_c))
        w_end.append(jnp.exp(cw_last))

    a_st, v_st, ab, ak, rb, rk, bkw = {}, {}, {}, {}, {}, {}, {}
    for un in units:
        c, pr = un
        rc, lp = rows(c), lanes(pr)
        ar = jnp.concatenate([at[rc, lp], rt[rc, lp]], axis=0).astype(BF16)
        bk_s = jnp.concatenate([_stack_heads(bt[rc, lp], lane_lo),
                                _stack_heads(kt[rc, lp], lane_lo)], axis=0)
        sc = _dot(ar, bk_s, NT)
        ab[un] = jnp.where(strict, sc[0:L, 0:PAIR], 0.0)
        ak[un] = jnp.where(strict, sc[0:L, PAIR:2 * PAIR], 0.0)
        rb[un] = jnp.where(incl, sc[L:2 * L, 0:PAIR], 0.0)
        rk[un] = jnp.where(incl, sc[L:2 * L, PAIR:2 * PAIR], 0.0)
        a_st[un] = _stack_heads(at[rc, lp], lane_lo)
        v_st[un] = _stack_heads(v[rc, lp], lane_lo)
        bkw[un] = jnp.concatenate([bv[rc, lp] * eend[c][:, lp], k2[rc, lp] * eend[c][:, lp]],
                                  axis=0).astype(BF16)

    tm = {un: eye2 + ab[un] for un in units}
    pw = {un: _dot(ab[un].astype(BF16), _stack_heads(ab[un], lane_lo), NN) for un in units}
    for it in range(5):
        for un in units:
            pw_s = _stack_heads(pw[un], lane_lo)
            if it < 4:
                both = _dot(jnp.concatenate([tm[un], pw[un]], axis=0).astype(BF16), pw_s, NN)
                tm[un] = tm[un] + both[0:L]
                pw[un] = both[L:2 * L]
            else:
                tm[un] = tm[un] + _dot(tm[un].astype(BF16), pw_s, NN)

    m2 = {un: _dot(jnp.concatenate([ak[un], rk[un]], axis=0).astype(BF16), v_st[un], NN) for un in units}
    rp, y0, mp, cc = {}, {}, {}, {}
    tau = {un: _dot(tm[un].astype(BF16),
                    jnp.concatenate([a_st[un], _stack_heads(m2[un][0:L], lane_lo)], axis=1), NN)
           for un in units}
    for un in units:
        c, pr = un
        ap, u0 = tau[un][:, 0:PAIR], tau[un][:, PAIR:2 * PAIR]
        ry = _dot(rb[un].astype(BF16),
                  jnp.concatenate([_stack_heads(ap, lane_lo), _stack_heads(u0, lane_lo)], axis=1), NN)
        rp[un] = (rt[rows(c), lanes(pr)] + ry[:, 0:PAIR]).astype(BF16)
        y0[un] = ry[:, PAIR:2 * PAIR] + m2[un][L:2 * L]
    for un in units:
        c, pr = un
        ap, u0 = tau[un][:, 0:PAIR], tau[un][:, PAIR:2 * PAIR]
        mp[un] = jnp.where(same_head, _dot(ap.astype(BF16), bkw[un][0:L], TN), 0.0).astype(BF16)
        uv = jnp.concatenate([u0, v[rows(c), lanes(pr)]], axis=0).astype(BF16)
        cc[un] = jnp.where(same_head, _dot(uv, bkw[un], TN), 0.0)

    state = [s_ref[pr] for pr in range(N_PAIRS)]
    y_rows = []
    for c in range(n_chunks):
        y_cols = []
        for pr in range(N_PAIRS):
            un = (c, pr)
            s = state[pr]
            sb = s.astype(BF16)
            y_cols.append(_dot(rp[un], sb, NT) + y0[un])
            state[pr] = s * w_end[c][:, lanes(pr)] + _dot(sb, mp[un], NN) + cc[un]
        y_rows.append(jnp.concatenate(y_cols, axis=1))
    for pr in range(N_PAIRS):
        s_ref[pr] = state[pr]
    y = jnp.concatenate(y_rows, axis=0)

    inv_n = 1.0 / HEAD_DIM
    mean = _head_sums(y, seg) * inv_n
    d = y - mean
    var = _head_sums(d * d, seg) * inv_n
    yn = d * lax.rsqrt(var + GN_EPS) * lnw_ref[...] + lnb_ref[...]
    bonus = _head_sums(r * k2 * rk_ref[...], seg) * v
    o_ref[0] = ((yn + bonus) * gate).astype(o_ref.dtype)


def _rwkv(p_rkv, p_lora, mu_rkv, mu_lora, w_lora, w0, a0, k_k, k_a, r_k, ln_w, ln_b, seg):
    b, t, _ = p_rkv.shape
    W = RWKV_WIDTH
    TB = RWKV_BLOCK
    row = lambda a: a.reshape(1, -1)
    const = lambda a: pl.BlockSpec(a.shape, lambda i, j: (0,) * a.ndim)
    consts = [row(mu_rkv), row(mu_lora), w_lora, row(w0), row(a0), row(k_k), row(k_a), row(r_k),
              row(ln_w), row(ln_b), seg]
    return pl.pallas_call(
        _rwkv_kernel,
        grid=(b, t // TB),
        in_specs=[pl.BlockSpec((1, TB, 3 * W), lambda i, j: (i, j, 0)),
                  pl.BlockSpec((1, TB, LORA_PAD), lambda i, j: (i, j, 0))] + [const(a) for a in consts],
        out_specs=pl.BlockSpec((1, TB, W), lambda i, j: (i, j, 0)),
        out_shape=jax.ShapeDtypeStruct((b, t, W), BF16),
        scratch_shapes=[pltpu.VMEM((N_PAIRS, PAIR, PAIR), F32),
                        pltpu.VMEM((1, 3 * W), F32), pltpu.VMEM((1, LORA_PAD), F32)],
        compiler_params=pltpu.CompilerParams(
            dimension_semantics=("parallel", "arbitrary"), vmem_limit_bytes=VMEM_LIMIT),
        name="rwkv",
    )(p_rkv, p_lora, *consts)


ATT_QBLOCK = 512
LOG2E = 1.4426950408889634


def _attn_kernel(sink_ref, q_ref, kvp_ref, kvc_ref, qn_ref, kn_ref, seg_ref, o_ref):
    blk = pl.program_id(1)
    QB = q_ref.shape[1]
    T = WINDOW
    inv_n = 1.0 / HEAD_DIM
    seg = seg_ref[...]

    q = q_ref[0]
    q = q * lax.rsqrt(_head_sums(q * q, seg) * inv_n + RMS_EPS) * qn_ref[...]
    kv = jnp.concatenate([kvp_ref[0], kvc_ref[0]], axis=0)
    k = kv[:, 0:KV_WIDTH]
    v = kv[:, KV_WIDTH:2 * KV_WIDTH]
    k = k * lax.rsqrt(_head_sums(k * k, seg) * inv_n + RMS_EPS) * kn_ref[...]

    lane_lo = lax.broadcasted_iota(jnp.int32, (1, KV_WIDTH), 1) < HEAD_DIM
    k_sw = pltpu.roll(k, HEAD_DIM, axis=1)
    v_sw = pltpu.roll(v, HEAD_DIM, axis=1)
    kd = (jnp.where(lane_lo, k, k_sw).astype(BF16), jnp.where(lane_lo, k_sw, k).astype(BF16))
    vd = (jnp.where(lane_lo, v, v_sw).astype(BF16), jnp.where(lane_lo, v_sw, v).astype(BF16))

    qi = lax.broadcasted_iota(jnp.int32, (T, 2 * T), 0)
    kj = lax.broadcasted_iota(jnp.int32, (T, 2 * T), 1)
    band = (kj <= qi + T) & (kj > qi + T - WINDOW)
    first = band & ((blk > 0) | (kj >= T))

    for i in range(QB // T):
        rq = slice(i * T, (i + 1) * T)
        rk = slice(i * T, (i + 2) * T)
        mask = first if i == 0 else band
        scores = []
        for g in range(ATT_KV_HEADS):
            parts = []
            for pr in (2 * g, 2 * g + 1):
                q_p = q[rq, pr * KV_WIDTH:(pr + 1) * KV_WIDTH]
                parts += [jnp.where(lane_lo, q_p, 0.0), jnp.where(lane_lo, 0.0, q_p)]
            lhs = jnp.concatenate(parts, axis=0).astype(BF16)
            scores.append(_dot(lhs, kd[g][rk], NT))
        outs = []
        for g in range(ATT_KV_HEADS):
            probs = []
            for hh in range(ATT_GROUP):
                s = jnp.where(mask, scores[g][hh * T:(hh + 1) * T], NEG_BIG)
                sink = sink_ref[g * ATT_GROUP + hh]
                m = jnp.maximum(jnp.max(s, axis=-1, keepdims=True), sink)
                pe = jnp.exp2(s - m)
                denom = jnp.sum(pe, axis=-1, keepdims=True) + jnp.exp2(sink - m)
                probs.append((pe * (1.0 / denom)).astype(BF16))
            o = _dot(jnp.concatenate(probs, axis=0), vd[g][rk], NN)
            outs += [jnp.where(lane_lo, o[0:T], o[T:2 * T]), jnp.where(lane_lo, o[2 * T:3 * T], o[3 * T:4 * T])]
        o_ref[0, rq, :] = jnp.concatenate(outs, axis=1).astype(o_ref.dtype)


def _attn(p_att, q_norm, k_norm, sinks, seg):
    b, t, _ = p_att.shape
    T = WINDOW
    QB = ATT_QBLOCK
    nq = ATT_WIDTH // (2 * KV_WIDTH)
    const = lambda a: pl.BlockSpec(a.shape, lambda i, j: (0,) * a.ndim)
    qn = (jnp.tile(q_norm, ATT_Q_HEADS) * (HEAD_DIM ** -0.5 * LOG2E)).reshape(1, ATT_WIDTH)
    kn = jnp.tile(k_norm, ATT_KV_HEADS).reshape(1, KV_WIDTH)
    return pl.pallas_call(
        _attn_kernel,
        grid=(b, t // QB),
        in_specs=[pl.BlockSpec(memory_space=pltpu.SMEM),
                  pl.BlockSpec((1, QB, ATT_WIDTH), lambda i, j: (i, j, 0)),
                  pl.BlockSpec((1, T, 2 * KV_WIDTH), lambda i, j: (i, jnp.maximum(j * (QB // T) - 1, 0), nq)),
                  pl.BlockSpec((1, QB, 2 * KV_WIDTH), lambda i, j: (i, j, nq)),
                  const(qn), const(kn), const(seg)],
        out_specs=pl.BlockSpec((1, QB, ATT_WIDTH), lambda i, j: (i, j, 0)),
        out_shape=jax.ShapeDtypeStruct((b, t, ATT_WIDTH), BF16),
        compiler_params=pltpu.CompilerParams(
            dimension_semantics=("parallel", "parallel"), vmem_limit_bytes=VMEM_LIMIT),
        name="attn",
    )(sinks * LOG2E, p_att, p_att, p_att, qn, kn, seg)


def _head_ones():
    i = jnp.arange(MXU_TILE) // HEAD_DIM
    return (i[:, None] == i[None, :]).astype(BF16)


def _layer(x, ffn1_norm, ffn1_w_gate, ffn1_w_up, ffn1_w_down, mix_norm, w_in,
           rwkv_mu, rwkv_w0, rwkv_w_lora_up, rwkv_a0, rwkv_a_lora_up, rwkv_g_lora_up,
           rwkv_k_k, rwkv_k_a, rwkv_r_k, rwkv_ln_w, rwkv_ln_b,
           attn_q_norm, attn_k_norm, attn_sinks,
           w_branch_rwkv, w_branch_attn, w_out,
           ffn2_norm, ffn2_w_gate, ffn2_w_up, ffn2_w_down, final_norm):
    b, t, d = x.shape
    n = b * t
    W = RWKV_WIDTH
    n_lora = DECAY_LORA + ICLR_LORA + GATE_LORA
    rwkv_cols = 3 * W + n_lora
    att_cols = ATT_WIDTH + 2 * KV_WIDTH

    bf = lambda a: a.astype(BF16)
    x2 = x.reshape(n, d)
    x2 = _ffn(x2, ffn1_norm, bf(ffn1_w_gate), bf(ffn1_w_up), bf(ffn1_w_down), tm=FFN_TILE)

    w_groups = (bf(w_in[:, :3 * W]),
                jnp.pad(bf(w_in[:, 3 * W:rwkv_cols]), ((0, 0), (0, LORA_PAD - n_lora))),
                bf(w_in[:, rwkv_cols:rwkv_cols + att_cols]),
                bf(w_in[:, rwkv_cols + att_cols:]))
    p_rkv, p_lora, p_att, p_gate = _proj(x2, mix_norm, w_groups)

    mu_rkv = rwkv_mu[:3 * W]
    mu_lora = jnp.concatenate([rwkv_mu[3 * W:], jnp.zeros((LORA_PAD - n_lora,), F32)])
    w_lora = jnp.zeros((LORA_PAD, 3 * W), F32)
    w_lora = w_lora.at[0:DECAY_LORA, 0:W].set(rwkv_w_lora_up)
    w_lora = w_lora.at[DECAY_LORA:DECAY_LORA + ICLR_LORA, W:2 * W].set(rwkv_a_lora_up)
    w_lora = w_lora.at[DECAY_LORA + ICLR_LORA:n_lora, 2 * W:3 * W].set(rwkv_g_lora_up)
    seg = _head_ones()
    y_rwkv = _rwkv(p_rkv.reshape(b, t, 3 * W), p_lora.reshape(b, t, LORA_PAD), mu_rkv, mu_lora, w_lora,
                   rwkv_w0, rwkv_a0, rwkv_k_k, rwkv_k_a, rwkv_r_k.reshape(-1), rwkv_ln_w, rwkv_ln_b,
                   seg)
    y_att = _attn(p_att.reshape(b, t, att_cols), attn_q_norm, attn_k_norm, attn_sinks,
                  seg)

    merge_args = (y_rwkv.reshape(n, W), y_att.reshape(n, ATT_WIDTH), p_gate,
                  bf(w_branch_rwkv), bf(w_branch_attn), bf(w_out))
    x2 = _ffn(x2, ffn2_norm, bf(ffn2_w_gate), bf(ffn2_w_up), bf(ffn2_w_down), final_norm, merge_args,
              tm=MERGE_FFN_TILE)
    return x2.reshape(b, t, d)


def kernel(x, ffn1_norm, ffn1_w_gate, ffn1_w_up, ffn1_w_down, mix_norm, w_in, rwkv_mu, rwkv_w0, rwkv_w_lora_up, rwkv_a0, rwkv_a_lora_up, rwkv_g_lora_up, rwkv_k_k, rwkv_k_a, rwkv_r_k, rwkv_ln_w, rwkv_ln_b, attn_q_norm, attn_k_norm, attn_sinks, w_branch_rwkv, w_branch_attn, w_out, ffn2_norm, ffn2_w_gate, ffn2_w_up, ffn2_w_down, final_norm):
    params = (ffn1_norm, ffn1_w_gate, ffn1_w_up, ffn1_w_down, mix_norm, w_in, rwkv_mu, rwkv_w0,
              rwkv_w_lora_up, rwkv_a0, rwkv_a_lora_up, rwkv_g_lora_up, rwkv_k_k, rwkv_k_a, rwkv_r_k,
              rwkv_ln_w, rwkv_ln_b, attn_q_norm, attn_k_norm, attn_sinks, w_branch_rwkv, w_branch_attn,
              w_out, ffn2_norm, ffn2_w_gate, ffn2_w_up, ffn2_w_down, final_norm)
    for layer in range(ffn1_norm.shape[0]):
        x = _layer(x, *(p[layer] for p in params))
    return x
```

```python
import functools

import jax
import jax.numpy as jnp
from jax import lax
from jax.experimental import pallas as pl
from jax.experimental.pallas import tpu as pltpu

F32 = jnp.float32
BF16 = jnp.bfloat16

HEAD_DIM = 64
RWKV_HEADS = 8
RWKV_WIDTH = RWKV_HEADS * HEAD_DIM
ATT_Q_HEADS = 8
ATT_KV_HEADS = 2
ATT_GROUP = ATT_Q_HEADS // ATT_KV_HEADS
ATT_WIDTH = ATT_Q_HEADS * HEAD_DIM
KV_WIDTH = ATT_KV_HEADS * HEAD_DIM
WINDOW = 128
DECAY_LORA = 32
ICLR_LORA = 32
GATE_LORA = 96
LORA_PAD = 256
RMS_EPS = 1e-6
GN_EPS = 64e-5
CHUNK = 64
EXP_M05 = 0.6065306597126334
NEG_BIG = -1e30

MXU_TILE = 256
VMEM_LIMIT = 56 * 1024 * 1024
FFN_TILE = 1024
MERGE_FFN_TILE = 1024


def _sigmoid(x):
    return 0.5 * jnp.tanh(0.5 * x) + 0.5


NN = ((1,), (0,))
NT = ((1,), (1,))
TN = ((0,), (0,))


HEAD_SUM_TERMS = 1


def _head_sums(x, seg, terms=HEAD_SUM_TERMS):
    rows, cols = x.shape
    width = min(cols, seg.shape[0])
    groups = cols // width
    parts, rem = [], x
    for i in range(terms):
        parts.append(rem.astype(BF16))
        if i + 1 < terms:
            rem = rem - parts[-1].astype(F32)
    stacked = jnp.concatenate([p[:, g * width:(g + 1) * width] for p in parts for g in range(groups)], axis=0)
    out = lax.dot_general(stacked, seg[0:width, 0:width], (NN, ((), ())), preferred_element_type=F32)
    blk = lambda i: out[i * rows:(i + 1) * rows]
    sums = []
    for g in range(groups):
        acc = blk(g)
        for i in range(1, terms):
            acc = acc + blk(i * groups + g)
        sums.append(acc)
    return jnp.concatenate(sums, axis=1)


def _mm_exact_lhs(a_bf16, b, passes=3):
    dn = (NN, ((), ()))
    out = None
    rem = b
    for _ in range(passes):
        part = rem.astype(BF16)
        term = lax.dot_general(a_bf16, part, dn, preferred_element_type=F32)
        out = term if out is None else out + term
        rem = rem - part.astype(F32)
    return out


def _ffn_kernel(*refs, final, merge):
    refs = list(refs)
    take = lambda k: [refs.pop(0) for _ in range(k)]
    (x_ref,) = take(1)
    if merge:
        yr_ref, ya_ref, gate_ref, wr_ref, wa_ref, wo_ref = take(6)
    g_ref, wg_ref, wu_ref, wd_ref = take(4)
    if final:
        (fg_ref,) = take(1)
    o_ref, h_ref, act_ref = refs

    x = x_ref[...]
    if merge:
        d = x.shape[1]
        gate = gate_ref[...].astype(F32)
        br = jnp.dot(yr_ref[...], wr_ref[...], preferred_element_type=F32)
        ba = jnp.dot(ya_ref[...], wa_ref[...], preferred_element_type=F32)
        merged = _sigmoid(gate[:, 0:d]) * br + _sigmoid(gate[:, d:2 * d]) * ba
        x = x + jnp.dot(merged.astype(BF16), wo_ref[...], preferred_element_type=F32)
        o_ref[...] = x
    ms = jnp.mean(x * x, axis=-1, keepdims=True)
    h_ref[...] = (x * lax.rsqrt(ms + RMS_EPS) * g_ref[...]).astype(BF16)
    dff = wg_ref.shape[1]
    for c in range(dff // MXU_TILE):
        sl = slice(c * MXU_TILE, (c + 1) * MXU_TILE)
        g = jnp.dot(h_ref[...], wg_ref[:, sl], preferred_element_type=F32)
        u = jnp.dot(h_ref[...], wu_ref[:, sl], preferred_element_type=F32)
        act_ref[:, sl] = (g * _sigmoid(g) * (0.5 * u)).astype(BF16)
    resid = o_ref[...] if merge else x_ref[...]
    y = resid + jnp.dot(act_ref[...], wd_ref[...], preferred_element_type=F32)
    if final:
        ms = jnp.mean(y * y, axis=-1, keepdims=True)
        y = y * lax.rsqrt(ms + RMS_EPS) * fg_ref[...]
    o_ref[...] = y


def _ffn(x, gain, wg, wu, wd, final_gain=None, merge_args=None, *, tm):
    n, d = x.shape
    dff = wg.shape[1]
    final = final_gain is not None
    merge = merge_args is not None
    tile = lambda a: pl.BlockSpec((tm, a.shape[1]), lambda i: (i, 0))
    row = lambda a: pl.BlockSpec((1, a.shape[-1]), lambda i: (0, 0))
    resident = lambda a: pl.BlockSpec(a.shape, lambda i: (0,) * a.ndim, pipeline_mode=pl.Buffered(1))
    in_specs, args = [tile(x)], [x]
    if merge:
        in_specs += [tile(a) for a in merge_args[:3]] + [resident(a) for a in merge_args[3:]]
        args += list(merge_args)
    in_specs += [row(gain), resident(wg), resident(wu), resident(wd)]
    args += [gain.reshape(1, d), wg, wu, wd]
    if final:
        in_specs.append(row(final_gain))
        args.append(final_gain.reshape(1, d))
    return pl.pallas_call(
        functools.partial(_ffn_kernel, final=final, merge=merge),
        grid=(n // tm,),
        in_specs=in_specs,
        out_specs=tile(x),
        out_shape=jax.ShapeDtypeStruct((n, d), F32),
        scratch_shapes=[pltpu.VMEM((tm, d), BF16), pltpu.VMEM((tm, dff), BF16)],
        compiler_params=pltpu.CompilerParams(
            dimension_semantics=("parallel",), vmem_limit_bytes=VMEM_LIMIT),
        name="merge_ffn" if merge else "ffn",
    )(*args)


def _proj_kernel(x_ref, g_ref, w_rkv_ref, w_lora_ref, w_att_ref, w_gate_ref, mu_rkv_ref, mu_lora_ref,
                 qn_ref, kn_ref, seg_ref, rkv_ref, act_ref, att_ref, gate_ref, prev_rkv_ref, prev_lora_ref,
                 *, tiles_per_seq):
    tm = x_ref.shape[0]
    first = pl.program_id(0) % tiles_per_seq == 0

    @pl.when(first)
    def _():
        prev_rkv_ref[...] = jnp.zeros_like(prev_rkv_ref)
        prev_lora_ref[...] = jnp.zeros_like(prev_lora_ref)

    x = x_ref[...]
    ms = jnp.mean(x * x, axis=-1, keepdims=True)
    h = (x * lax.rsqrt(ms + RMS_EPS) * g_ref[...]).astype(BF16)
    row = lax.broadcasted_iota(jnp.int32, (tm, 1), 0)

    def lerp(p, prev_ref, mu):
        shifted = jnp.where(row == 0, prev_ref[...], pltpu.roll(p, 1, axis=0))
        prev_ref[...] = p[tm - 1:tm, :]
        return p + (shifted - p) * mu

    rkv_ref[...] = lerp(jnp.dot(h, w_rkv_ref[...], preferred_element_type=F32), prev_rkv_ref, mu_rkv_ref[...])

    pl_ = lerp(jnp.dot(h, w_lora_ref[...], preferred_element_type=F32), prev_lora_ref, mu_lora_ref[...])
    lane = lax.broadcasted_iota(jnp.int32, (1, LORA_PAD), 1)
    act = jnp.where(lane < DECAY_LORA, jnp.tanh(pl_),
                    jnp.where(lane < DECAY_LORA + ICLR_LORA, pl_,
                              jnp.where(lane < DECAY_LORA + ICLR_LORA + GATE_LORA, _sigmoid(pl_), 0.0)))
    act_ref[...] = act.astype(act_ref.dtype)

    a = jnp.dot(h, w_att_ref[...], preferred_element_type=F32)
    inv_n = 1.0 / HEAD_DIM
    seg = seg_ref[...]
    q = a[:, 0:ATT_WIDTH]
    k = a[:, ATT_WIDTH:ATT_WIDTH + KV_WIDTH]
    v = a[:, ATT_WIDTH + KV_WIDTH:ATT_WIDTH + 2 * KV_WIDTH]
    q = q * lax.rsqrt(_head_sums(q * q, seg) * inv_n + RMS_EPS) * qn_ref[...]
    k = k * lax.rsqrt(_head_sums(k * k, seg) * inv_n + RMS_EPS) * kn_ref[...]
    lane_lo = lax.broadcasted_iota(jnp.int32, (1, KV_WIDTH), 1) < HEAD_DIM
    k_sw = pltpu.roll(k, HEAD_DIM, axis=1)
    v_sw = pltpu.roll(v, HEAD_DIM, axis=1)
    att_ref[...] = jnp.concatenate(
        [q, jnp.where(lane_lo, k, k_sw), jnp.where(lane_lo, k_sw, k),
         jnp.where(lane_lo, v, v_sw), jnp.where(lane_lo, v_sw, v)], axis=1).astype(att_ref.dtype)

    gate_ref[...] = jnp.dot(h, w_gate_ref[...], preferred_element_type=F32).astype(gate_ref.dtype)


ATT_COLS_OUT = ATT_WIDTH + 4 * KV_WIDTH


def _proj(x, gain, weights, mu_rkv, mu_lora, qn, kn, seg, *, seq_len, tm=512):
    n, d = x.shape
    W = RWKV_WIDTH
    resident = lambda a: pl.BlockSpec(a.shape, lambda i: (0, 0), pipeline_mode=pl.Buffered(1))
    row = lambda a: a.reshape(1, -1)
    smalls = [row(mu_rkv), row(mu_lora), row(qn), row(kn), seg]
    outs = [(3 * W, F32), (LORA_PAD, BF16), (ATT_COLS_OUT, BF16), (weights[3].shape[1], BF16)]
    return pl.pallas_call(
        functools.partial(_proj_kernel, tiles_per_seq=seq_len // tm),
        grid=(n // tm,),
        in_specs=[pl.BlockSpec((tm, d), lambda i: (i, 0)), pl.BlockSpec((1, d), lambda i: (0, 0))]
        + [resident(w) for w in weights] + [resident(a) for a in smalls],
        out_specs=[pl.BlockSpec((tm, c), lambda i: (i, 0)) for c, _ in outs],
        out_shape=[jax.ShapeDtypeStruct((n, c), dt) for c, dt in outs],
        scratch_shapes=[pltpu.VMEM((1, 3 * W), F32), pltpu.VMEM((1, LORA_PAD), F32)],
        compiler_params=pltpu.CompilerParams(
            dimension_semantics=("arbitrary",), vmem_limit_bytes=VMEM_LIMIT),
        name="proj",
    )(x, gain.reshape(1, d), *weights, *smalls)


PAIR = 2 * HEAD_DIM
N_PAIRS = RWKV_HEADS // 2
RWKV_BLOCK = 512
CUMSUM_PASSES = 2
RWKV_GROUP = 4


def _dot(a, b, dims):
    return lax.dot_general(a, b, (dims, ((), ())), preferred_element_type=F32)


def _stack_heads(x, lane_lo):
    return jnp.concatenate([jnp.where(lane_lo, x, 0.0), jnp.where(lane_lo, 0.0, x)], axis=0).astype(BF16)


def _mixer_kernel(sink_ref, rkv_ref, act_ref, q_ref, kvp_ref, kvc_ref, wl_ref, w0_ref, a0_ref, kk_ref, ka_ref,
                  rk_ref, lnw_ref, lnb_ref, seg_ref, o_ref, oa_ref, s_ref):
    t = pl.program_id(1)
    TB = rkv_ref.shape[1]
    L = CHUNK
    W = RWKV_WIDTH

    @pl.when(t == 0)
    def _():
        s_ref[...] = jnp.zeros_like(s_ref)

    p = rkv_ref[0]
    r = p[:, 0:W]
    k = p[:, W:2 * W]
    v = p[:, 2 * W:3 * W]
    z = _dot(act_ref[0], wl_ref[...], NN)
    e = EXP_M05 * _sigmoid(z[:, 0:W] + w0_ref[...])
    a_ic = _sigmoid(z[:, W:2 * W] + a0_ref[...])
    gate = z[:, 2 * W:3 * W]

    seg = seg_ref[...]
    kk = k * kk_ref[...]
    kk = kk * lax.rsqrt(jnp.maximum(_head_sums(kk * kk, seg), 1e-24))
    k2 = k * (1.0 + (a_ic - 1.0) * ka_ref[...])
    bv = kk * a_ic

    ci = lax.broadcasted_iota(jnp.int32, (MXU_TILE, MXU_TILE), 0)
    cj = lax.broadcasted_iota(jnp.int32, (MXU_TILE, MXU_TILE), 1)
    tri = jnp.where((cj <= ci) & ((ci // L) == (cj // L)), 1.0, 0.0).astype(BF16)
    cw = -jnp.concatenate([_mm_exact_lhs(tri, e[i:i + MXU_TILE], CUMSUM_PASSES)
                           for i in range(0, TB, MXU_TILE)], axis=0)
    at = -kk * jnp.exp(cw + e)
    rt = r * jnp.exp(cw)
    einv = jnp.exp(-cw)
    bt = bv * einv
    kt = k2 * einv

    ti = lax.broadcasted_iota(jnp.int32, (L, PAIR), 0)
    si = lax.broadcasted_iota(jnp.int32, (L, PAIR), 1) % HEAD_DIM
    strict = si < ti
    incl = si <= ti
    eye2 = jnp.where(si == ti, 1.0, 0.0)
    lane_lo = lax.broadcasted_iota(jnp.int32, (1, PAIR), 1) < HEAD_DIM
    bi = lax.broadcasted_iota(jnp.int32, (PAIR, PAIR), 0) // HEAD_DIM
    bj = lax.broadcasted_iota(jnp.int32, (PAIR, PAIR), 1) // HEAD_DIM
    same_head = bi == bj

    n_chunks = TB // L
    rows = lambda c: slice(c * L, (c + 1) * L)
    lanes = lambda pr: slice(pr * PAIR, (pr + 1) * PAIR)
    inv_n = 1.0 / HEAD_DIM

    def prepare(chunks, out):
        units = [(c, pr) for c in chunks for pr in range(N_PAIRS)]
        w_end, eend = {}, {}
        for c in chunks:
            cw_c = cw[rows(c)]
            cw_last = cw_c[L - 1:L, :]
            eend[c] = jnp.exp(cw_last - cw_c)
            w_end[c] = jnp.exp(cw_last)
        a_st, v_st, ab, ak, rb, rk, bkw = {}, {}, {}, {}, {}, {}, {}
        for un in units:
            c, pr = un
            rc, lp = rows(c), lanes(pr)
            ar = jnp.concatenate([at[rc, lp], rt[rc, lp]], axis=0).astype(BF16)
            bk_s = jnp.concatenate([_stack_heads(bt[rc, lp], lane_lo),
                                    _stack_heads(kt[rc, lp], lane_lo)], axis=0)
            sc = _dot(ar, bk_s, NT)
            ab[un] = jnp.where(strict, sc[0:L, 0:PAIR], 0.0)
            ak[un] = jnp.where(strict, sc[0:L, PAIR:2 * PAIR], 0.0)
            rb[un] = jnp.where(incl, sc[L:2 * L, 0:PAIR], 0.0)
            rk[un] = jnp.where(incl, sc[L:2 * L, PAIR:2 * PAIR], 0.0)
            a_st[un] = _stack_heads(at[rc, lp], lane_lo)
            v_st[un] = _stack_heads(v[rc, lp], lane_lo)
            bkw[un] = jnp.concatenate([bv[rc, lp] * eend[c][:, lp], k2[rc, lp] * eend[c][:, lp]],
                                      axis=0).astype(BF16)
        yield
        tm = {un: eye2 + ab[un] for un in units}
        pw = {un: _dot(ab[un].astype(BF16), _stack_heads(ab[un], lane_lo), NN) for un in units}
        yield
        for it in range(5):
            for un in units:
                pw_s = _stack_heads(pw[un], lane_lo)
                if it < 4:
                    both = _dot(jnp.concatenate([tm[un], pw[un]], axis=0).astype(BF16), pw_s, NN)
                    tm[un] = tm[un] + both[0:L]
                    pw[un] = both[L:2 * L]
                else:
                    tm[un] = tm[un] + _dot(tm[un].astype(BF16), pw_s, NN)
            yield
        m2 = {un: _dot(jnp.concatenate([ak[un], rk[un]], axis=0).astype(BF16), v_st[un], NN) for un in units}
        yield
        tau = {un: _dot(tm[un].astype(BF16),
                        jnp.concatenate([a_st[un], _stack_heads(m2[un][0:L], lane_lo)], axis=1), NN)
               for un in units}
        yield
        for un in units:
            c, pr = un
            ap, u0 = tau[un][:, 0:PAIR], tau[un][:, PAIR:2 * PAIR]
            ry = _dot(rb[un].astype(BF16),
                      jnp.concatenate([_stack_heads(ap, lane_lo), _stack_heads(u0, lane_lo)], axis=1), NN)
            out[un] = dict(rp=(rt[rows(c), lanes(pr)] + ry[:, 0:PAIR]).astype(BF16),
                           y0=ry[:, PAIR:2 * PAIR] + m2[un][L:2 * L], w_end=w_end[c][:, lanes(pr)])
        yield
        for un in units:
            c, pr = un
            ap, u0 = tau[un][:, 0:PAIR], tau[un][:, PAIR:2 * PAIR]
            out[un]["mp"] = jnp.where(same_head, _dot(ap.astype(BF16), bkw[un][0:L], TN), 0.0).astype(BF16)
            uv = jnp.concatenate([u0, v[rows(c), lanes(pr)]], axis=0).astype(BF16)
            out[un]["cc"] = jnp.where(same_head, _dot(uv, bkw[un], TN), 0.0)

    def recur(chunks, prep, state, y_rows):
        for c in chunks:
            y_cols = []
            for pr in range(N_PAIRS):
                u = prep[(c, pr)]
                sb = state[pr].astype(BF16)
                y_cols.append(_dot(u["rp"], sb, NT) + u["y0"])
                state[pr] = state[pr] * u["w_end"] + _dot(sb, u["mp"], NN) + u["cc"]
            y_rows[c] = jnp.concatenate(y_cols, axis=1)
            yield

    def finish(chunks, y_rows):
        rg = slice(chunks[0] * L, (chunks[-1] + 1) * L)
        y = jnp.concatenate([y_rows[c] for c in chunks], axis=0)
        mean = _head_sums(y, seg) * inv_n
        yield
        d = y - mean
        var = _head_sums(d * d, seg) * inv_n
        yield
        yn = d * lax.rsqrt(var + GN_EPS) * lnw_ref[...] + lnb_ref[...]
        bonus = _head_sums(r[rg] * k2[rg] * rk_ref[...], seg) * v[rg]
        o_ref[0, rg, :] = ((yn + bonus) * gate[rg]).astype(o_ref.dtype)

    def interleave(*gens):
        gens = list(gens)
        while gens:
            for g in list(gens):
                if next(g, StopIteration) is StopIteration:
                    gens.remove(g)

    groups = [list(range(g * RWKV_GROUP, (g + 1) * RWKV_GROUP)) for g in range(n_chunks // RWKV_GROUP)]
    state = [s_ref[pr] for pr in range(N_PAIRS)]
    y_rows = {}
    preps = [dict() for _ in groups]
    attn = _attn_stages(sink_ref, q_ref, kvp_ref, kvc_ref, oa_ref, t)
    interleave(prepare(groups[0], preps[0]), attn)
    for gi, chunks in enumerate(groups):
        partners = [recur(chunks, preps[gi], state, y_rows), attn]
        if gi + 1 < len(groups):
            partners.append(prepare(groups[gi + 1], preps[gi + 1]))
        if gi > 0:
            partners.append(finish(groups[gi - 1], y_rows))
        interleave(*partners)
    interleave(finish(groups[-1], y_rows), attn)
    for pr in range(N_PAIRS):
        s_ref[pr] = state[pr]


def _mixers(p_rkv, act, p_att, sinks, w_lora, w0, a0, k_k, k_a, r_k, ln_w, ln_b, seg):
    b, t, _ = p_rkv.shape
    W = RWKV_WIDTH
    TB = RWKV_BLOCK
    T = WINDOW
    kvw = 4 * KV_WIDTH
    nq = ATT_WIDTH // kvw
    row = lambda a: a.reshape(1, -1)
    const = lambda a: pl.BlockSpec(a.shape, lambda i, j: (0,) * a.ndim)
    blk = lambda width: pl.BlockSpec((1, TB, width), lambda i, j: (i, j, 0))
    consts = [w_lora, row(w0), row(a0), row(k_k), row(k_a), row(r_k), row(ln_w), row(ln_b), seg]
    return pl.pallas_call(
        _mixer_kernel,
        grid=(b, t // TB),
        in_specs=[pl.BlockSpec(memory_space=pltpu.SMEM), blk(3 * W), blk(LORA_PAD), blk(ATT_WIDTH),
                  pl.BlockSpec((1, T, kvw), lambda i, j: (i, jnp.maximum(j * (TB // T) - 1, 0), nq)),
                  pl.BlockSpec((1, TB, kvw), lambda i, j: (i, j, nq))] + [const(a) for a in consts],
        out_specs=[blk(W), blk(ATT_WIDTH)],
        out_shape=[jax.ShapeDtypeStruct((b, t, W), BF16), jax.ShapeDtypeStruct((b, t, ATT_WIDTH), BF16)],
        scratch_shapes=[pltpu.VMEM((N_PAIRS, PAIR, PAIR), F32)],
        compiler_params=pltpu.CompilerParams(
            dimension_semantics=("parallel", "arbitrary"), vmem_limit_bytes=VMEM_LIMIT),
        name="mixers",
    )(sinks * LOG2E, p_rkv, act, p_att, p_att, p_att, *consts)


LOG2E = 1.4426950408889634


def _attn_stages(sink_ref, q_ref, kvp_ref, kvc_ref, o_ref, blk):
    QB = q_ref.shape[1]
    T = WINDOW
    q = q_ref[0]
    kv = jnp.concatenate([kvp_ref[0], kvc_ref[0]], axis=0)
    kd = [kv[:, g * KV_WIDTH:(g + 1) * KV_WIDTH] for g in range(ATT_KV_HEADS)]
    vd = [kv[:, (ATT_KV_HEADS + g) * KV_WIDTH:(ATT_KV_HEADS + g + 1) * KV_WIDTH] for g in range(ATT_KV_HEADS)]
    lane_lo = lax.broadcasted_iota(jnp.int32, (1, KV_WIDTH), 1) < HEAD_DIM
    zero = jnp.zeros((), q.dtype)

    qi = lax.broadcasted_iota(jnp.int32, (T, 2 * T), 0)
    kj = lax.broadcasted_iota(jnp.int32, (T, 2 * T), 1)
    band = (kj <= qi + T) & (kj > qi + T - WINDOW)
    first = band & ((blk > 0) | (kj >= T))

    for i in range(QB // T):
        rq = slice(i * T, (i + 1) * T)
        rk = slice(i * T, (i + 2) * T)
        mask = first if i == 0 else band
        scores = []
        for g in range(ATT_KV_HEADS):
            parts = []
            for pr in (2 * g, 2 * g + 1):
                q_p = q[rq, pr * KV_WIDTH:(pr + 1) * KV_WIDTH]
                parts += [jnp.where(lane_lo, q_p, zero), jnp.where(lane_lo, zero, q_p)]
            lhs = jnp.concatenate(parts, axis=0)
            scores.append(_dot(lhs, kd[g][rk], NT))
        yield
        outs = []
        for g in range(ATT_KV_HEADS):
            probs = []
            for hh in range(ATT_GROUP):
                s = jnp.where(mask, scores[g][hh * T:(hh + 1) * T], NEG_BIG)
                sink = sink_ref[g * ATT_GROUP + hh]
                m = jnp.maximum(jnp.max(s, axis=-1, keepdims=True), sink)
                pe = jnp.exp2(s - m)
                denom = jnp.sum(pe, axis=-1, keepdims=True) + jnp.exp2(sink - m)
                probs.append((pe * (1.0 / denom)).astype(BF16))
            o = _dot(jnp.concatenate(probs, axis=0), vd[g][rk], NN)
            outs += [jnp.where(lane_lo, o[0:T], o[T:2 * T]), jnp.where(lane_lo, o[2 * T:3 * T], o[3 * T:4 * T])]
            yield
        o_ref[0, rq, :] = jnp.concatenate(outs, axis=1).astype(o_ref.dtype)


def _head_ones():
    i = jnp.arange(MXU_TILE) // HEAD_DIM
    return (i[:, None] == i[None, :]).astype(BF16)


def _layer(x, ffn1_norm, ffn1_w_gate, ffn1_w_up, ffn1_w_down, mix_norm, w_in,
           rwkv_mu, rwkv_w0, rwkv_w_lora_up, rwkv_a0, rwkv_a_lora_up, rwkv_g_lora_up,
           rwkv_k_k, rwkv_k_a, rwkv_r_k, rwkv_ln_w, rwkv_ln_b,
           attn_q_norm, attn_k_norm, attn_sinks,
           w_branch_rwkv, w_branch_attn, w_out,
           ffn2_norm, ffn2_w_gate, ffn2_w_up, ffn2_w_down, final_norm):
    b, t, d = x.shape
    n = b * t
    W = RWKV_WIDTH
    n_lora = DECAY_LORA + ICLR_LORA + GATE_LORA
    rwkv_cols = 3 * W + n_lora
    att_cols = ATT_WIDTH + 2 * KV_WIDTH

    bf = lambda a: a.astype(BF16)
    x2 = x.reshape(n, d)
    x2 = _ffn(x2, ffn1_norm, bf(ffn1_w_gate), bf(ffn1_w_up), bf(ffn1_w_down), tm=FFN_TILE)

    w_groups = (bf(w_in[:, :3 * W]),
                jnp.pad(bf(w_in[:, 3 * W:rwkv_cols]), ((0, 0), (0, LORA_PAD - n_lora))),
                bf(w_in[:, rwkv_cols:rwkv_cols + att_cols]),
                bf(w_in[:, rwkv_cols + att_cols:]))
    mu_rkv = rwkv_mu[:3 * W]
    mu_lora = jnp.concatenate([rwkv_mu[3 * W:], jnp.zeros((LORA_PAD - n_lora,), F32)])
    qn = jnp.tile(attn_q_norm, ATT_Q_HEADS) * (HEAD_DIM ** -0.5 * LOG2E)
    kn = jnp.tile(attn_k_norm, ATT_KV_HEADS)
    seg = _head_ones()
    p_rkv, act, p_att, p_gate = _proj(x2, mix_norm, w_groups, mu_rkv, mu_lora, qn, kn, seg, seq_len=t)

    w_lora = jnp.zeros((LORA_PAD, 3 * W), F32)
    w_lora = w_lora.at[0:DECAY_LORA, 0:W].set(rwkv_w_lora_up)
    w_lora = w_lora.at[DECAY_LORA:DECAY_LORA + ICLR_LORA, W:2 * W].set(rwkv_a_lora_up)
    w_lora = w_lora.at[DECAY_LORA + ICLR_LORA:n_lora, 2 * W:3 * W].set(rwkv_g_lora_up)
    y_rwkv, y_att = _mixers(p_rkv.reshape(b, t, 3 * W), act.reshape(b, t, LORA_PAD),
                            p_att.reshape(b, t, ATT_COLS_OUT), attn_sinks, bf(w_lora),
                            rwkv_w0, rwkv_a0, rwkv_k_k, rwkv_k_a, rwkv_r_k.reshape(-1), rwkv_ln_w, rwkv_ln_b, seg)

    merge_args = (y_rwkv.reshape(n, W), y_att.reshape(n, ATT_WIDTH), p_gate,
                  bf(w_branch_rwkv), bf(w_branch_attn), bf(w_out))
    x2 = _ffn(x2, ffn2_norm, bf(ffn2_w_gate), bf(ffn2_w_up), bf(ffn2_w_down), final_norm, merge_args,
              tm=MERGE_FFN_TILE)
    return x2.reshape(b, t, d)


def kernel(x, ffn1_norm, ffn1_w_gate, ffn1_w_up, ffn1_w_down, mix_norm, w_in, rwkv_mu, rwkv_w0, rwkv_w_lora_up, rwkv_a0, rwkv_a_lora_up, rwkv_g_lora_up, rwkv_k_k, rwkv_k_a, rwkv_r_k, rwkv_ln_w, rwkv_ln_b, attn_q_norm, attn_k_norm, attn_sinks, w_branch_rwkv, w_branch_attn, w_out, ffn2_norm, ffn2_w_gate, ffn2_w_up, ffn2_w_down, final_norm):
    params = (ffn1_norm, ffn1_w_gate, ffn1_w_up, ffn1_w_down, mix_norm, w_in, rwkv_mu, rwkv_w0,
              rwkv_w_lora_up, rwkv_a0, rwkv_a_lora_up, rwkv_g_lora_up, rwkv_k_k, rwkv_k_a, rwkv_r_k,
              rwkv_ln_w, rwkv_ln_b, attn_q_norm, attn_k_norm, attn_sinks, w_branch_rwkv, w_branch_attn,
              w_out, ffn2_norm, ffn2_w_gate, ffn2_w_up, ffn2_w_down, final_norm)
    for layer in range(ffn1_norm.shape[0]):
        x = _layer(x, *(p[layer] for p in params))
    return x
```

```python
import functools

import jax
import jax.numpy as jnp
from jax import lax
from jax.experimental import pallas as pl
from jax.experimental.pallas import tpu as pltpu

F32 = jnp.float32
BF16 = jnp.bfloat16

HEAD_DIM = 64
RWKV_HEADS = 8
RWKV_WIDTH = RWKV_HEADS * HEAD_DIM
ATT_Q_HEADS = 8
ATT_KV_HEADS = 2
ATT_GROUP = ATT_Q_HEADS // ATT_KV_HEADS
ATT_WIDTH = ATT_Q_HEADS * HEAD_DIM
KV_WIDTH = ATT_KV_HEADS * HEAD_DIM
WINDOW = 128
DECAY_LORA = 32
ICLR_LORA = 32
GATE_LORA = 96
LORA_PAD = 256
RMS_EPS = 1e-6
GN_EPS = 64e-5
CHUNK = 64
EXP_M05 = 0.6065306597126334
NEG_BIG = -1e30

MXU_TILE = 256
VMEM_LIMIT = 56 * 1024 * 1024
FFN_TILE = 1024
MERGE_FFN_TILE = 1024


def _sigmoid(x):
    return 0.5 * jnp.tanh(0.5 * x) + 0.5


NN = ((1,), (0,))
NT = ((1,), (1,))
TN = ((0,), (0,))


HEAD_SUM_TERMS = 1


def _head_sums(x, seg, terms=HEAD_SUM_TERMS):
    rows, cols = x.shape
    width = min(cols, seg.shape[0])
    groups = cols // width
    parts, rem = [], x
    for i in range(terms):
        parts.append(rem.astype(BF16))
        if i + 1 < terms:
            rem = rem - parts[-1].astype(F32)
    stacked = jnp.concatenate([p[:, g * width:(g + 1) * width] for p in parts for g in range(groups)], axis=0)
    out = lax.dot_general(stacked, seg[0:width, 0:width], (NN, ((), ())), preferred_element_type=F32)
    blk = lambda i: out[i * rows:(i + 1) * rows]
    sums = []
    for g in range(groups):
        acc = blk(g)
        for i in range(1, terms):
            acc = acc + blk(i * groups + g)
        sums.append(acc)
    return jnp.concatenate(sums, axis=1)


def _mm_exact_lhs(a_bf16, b, passes=3):
    dn = (NN, ((), ()))
    out = None
    rem = b
    for _ in range(passes):
        part = rem.astype(BF16)
        term = lax.dot_general(a_bf16, part, dn, preferred_element_type=F32)
        out = term if out is None else out + term
        rem = rem - part.astype(F32)
    return out


def _ffn_kernel(*refs, final, merge):
    refs = list(refs)
    take = lambda k: [refs.pop(0) for _ in range(k)]
    (x_ref,) = take(1)
    if merge:
        yr_ref, ya_ref, gate_ref, wr_ref, wa_ref, wo_ref = take(6)
    g_ref, wg_ref, wu_ref, wd_ref = take(4)
    if final:
        (fg_ref,) = take(1)
    o_ref, h_ref, act_ref = refs

    x = x_ref[...]
    if merge:
        d = x.shape[1]
        gate = gate_ref[...].astype(F32)
        br = jnp.dot(yr_ref[...], wr_ref[...], preferred_element_type=F32)
        ba = jnp.dot(ya_ref[...], wa_ref[...], preferred_element_type=F32)
        merged = _sigmoid(gate[:, 0:d]) * br + _sigmoid(gate[:, d:2 * d]) * ba
        x = x + jnp.dot(merged.astype(BF16), wo_ref[...], preferred_element_type=F32)
        o_ref[...] = x
    ms = jnp.mean(x * x, axis=-1, keepdims=True)
    h_ref[...] = (x * lax.rsqrt(ms + RMS_EPS) * g_ref[...]).astype(BF16)
    dff = wg_ref.shape[1]
    for c in range(dff // MXU_TILE):
        sl = slice(c * MXU_TILE, (c + 1) * MXU_TILE)
        g = jnp.dot(h_ref[...], wg_ref[:, sl], preferred_element_type=F32)
        u = jnp.dot(h_ref[...], wu_ref[:, sl], preferred_element_type=F32)
        act_ref[:, sl] = (g * _sigmoid(g) * (0.5 * u)).astype(BF16)
    resid = o_ref[...] if merge else x_ref[...]
    y = resid + jnp.dot(act_ref[...], wd_ref[...], preferred_element_type=F32)
    if final:
        ms = jnp.mean(y * y, axis=-1, keepdims=True)
        y = y * lax.rsqrt(ms + RMS_EPS) * fg_ref[...]
    o_ref[...] = y


def _ffn(x, gain, wg, wu, wd, final_gain=None, merge_args=None, *, tm):
    n, d = x.shape
    dff = wg.shape[1]
    final = final_gain is not None
    merge = merge_args is not None
    tile = lambda a: pl.BlockSpec((tm, a.shape[1]), lambda i: (i, 0))
    row = lambda a: pl.BlockSpec((1, a.shape[-1]), lambda i: (0, 0))
    resident = lambda a: pl.BlockSpec(a.shape, lambda i: (0,) * a.ndim, pipeline_mode=pl.Buffered(1))
    in_specs, args = [tile(x)], [x]
    if merge:
        in_specs += [tile(a) for a in merge_args[:3]] + [resident(a) for a in merge_args[3:]]
        args += list(merge_args)
    in_specs += [row(gain), resident(wg), resident(wu), resident(wd)]
    args += [gain.reshape(1, d), wg, wu, wd]
    if final:
        in_specs.append(row(final_gain))
        args.append(final_gain.reshape(1, d))
    return pl.pallas_call(
        functools.partial(_ffn_kernel, final=final, merge=merge),
        grid=(n // tm,),
        in_specs=in_specs,
        out_specs=tile(x),
        out_shape=jax.ShapeDtypeStruct((n, d), F32),
        scratch_shapes=[pltpu.VMEM((tm, d), BF16), pltpu.VMEM((tm, dff), BF16)],
        compiler_params=pltpu.CompilerParams(
            dimension_semantics=("parallel",), vmem_limit_bytes=VMEM_LIMIT),
        name="merge_ffn" if merge else "ffn",
    )(*args)


def _proj_kernel(x_ref, g_ref, w_rkv_ref, w_lora_ref, w_att_ref, w_gate_ref, mu_rkv_ref, mu_lora_ref,
                 qn_ref, kn_ref, seg_ref, rkv_ref, act_ref, att_ref, gate_ref, prev_rkv_ref, prev_lora_ref,
                 *, tiles_per_seq):
    tm = x_ref.shape[0]
    first = pl.program_id(0) % tiles_per_seq == 0

    @pl.when(first)
    def _():
        prev_rkv_ref[...] = jnp.zeros_like(prev_rkv_ref)
        prev_lora_ref[...] = jnp.zeros_like(prev_lora_ref)

    x = x_ref[...]
    ms = jnp.mean(x * x, axis=-1, keepdims=True)
    h = (x * lax.rsqrt(ms + RMS_EPS) * g_ref[...]).astype(BF16)
    row = lax.broadcasted_iota(jnp.int32, (tm, 1), 0)

    def lerp(p, prev_ref, mu):
        shifted = jnp.where(row == 0, prev_ref[...], pltpu.roll(p, 1, axis=0))
        prev_ref[...] = p[tm - 1:tm, :]
        return p + (shifted - p) * mu

    rkv_ref[...] = lerp(jnp.dot(h, w_rkv_ref[...], preferred_element_type=F32), prev_rkv_ref, mu_rkv_ref[...])

    pl_ = lerp(jnp.dot(h, w_lora_ref[...], preferred_element_type=F32), prev_lora_ref, mu_lora_ref[...])
    lane = lax.broadcasted_iota(jnp.int32, (1, LORA_PAD), 1)
    act = jnp.where(lane < DECAY_LORA, jnp.tanh(pl_),
                    jnp.where(lane < DECAY_LORA + ICLR_LORA, pl_,
                              jnp.where(lane < DECAY_LORA + ICLR_LORA + GATE_LORA, _sigmoid(pl_), 0.0)))
    act_ref[...] = act.astype(act_ref.dtype)

    a = jnp.dot(h, w_att_ref[...], preferred_element_type=F32)
    inv_n = 1.0 / HEAD_DIM
    seg = seg_ref[...]
    q = a[:, 0:ATT_WIDTH]
    k = a[:, ATT_WIDTH:ATT_WIDTH + KV_WIDTH]
    v = a[:, ATT_WIDTH + KV_WIDTH:ATT_WIDTH + 2 * KV_WIDTH]
    q = q * lax.rsqrt(_head_sums(q * q, seg) * inv_n + RMS_EPS) * qn_ref[...]
    k = k * lax.rsqrt(_head_sums(k * k, seg) * inv_n + RMS_EPS) * kn_ref[...]
    lane_lo = lax.broadcasted_iota(jnp.int32, (1, KV_WIDTH), 1) < HEAD_DIM
    k_sw = pltpu.roll(k, HEAD_DIM, axis=1)
    v_sw = pltpu.roll(v, HEAD_DIM, axis=1)
    att_ref[...] = jnp.concatenate(
        [q, jnp.where(lane_lo, k, k_sw), jnp.where(lane_lo, k_sw, k),
         jnp.where(lane_lo, v, v_sw), jnp.where(lane_lo, v_sw, v)], axis=1).astype(att_ref.dtype)

    gate_ref[...] = jnp.dot(h, w_gate_ref[...], preferred_element_type=F32).astype(gate_ref.dtype)


ATT_COLS_OUT = ATT_WIDTH + 4 * KV_WIDTH


def _proj(x, gain, weights, mu_rkv, mu_lora, qn, kn, seg, *, seq_len, tm=1024):
    n, d = x.shape
    W = RWKV_WIDTH
    resident = lambda a: pl.BlockSpec(a.shape, lambda i: (0, 0), pipeline_mode=pl.Buffered(1))
    row = lambda a: a.reshape(1, -1)
    smalls = [row(mu_rkv), row(mu_lora), row(qn), row(kn), seg]
    outs = [(3 * W, F32), (LORA_PAD, BF16), (ATT_COLS_OUT, BF16), (weights[3].shape[1], BF16)]
    return pl.pallas_call(
        functools.partial(_proj_kernel, tiles_per_seq=seq_len // tm),
        grid=(n // tm,),
        in_specs=[pl.BlockSpec((tm, d), lambda i: (i, 0)), pl.BlockSpec((1, d), lambda i: (0, 0))]
        + [resident(w) for w in weights] + [resident(a) for a in smalls],
        out_specs=[pl.BlockSpec((tm, c), lambda i: (i, 0)) for c, _ in outs],
        out_shape=[jax.ShapeDtypeStruct((n, c), dt) for c, dt in outs],
        scratch_shapes=[pltpu.VMEM((1, 3 * W), F32), pltpu.VMEM((1, LORA_PAD), F32)],
        compiler_params=pltpu.CompilerParams(
            dimension_semantics=("arbitrary",), vmem_limit_bytes=VMEM_LIMIT),
        name="proj",
    )(x, gain.reshape(1, d), *weights, *smalls)


PAIR = 2 * HEAD_DIM
N_PAIRS = RWKV_HEADS // 2
RWKV_BLOCK = 512
CUMSUM_PASSES = 2
RWKV_GROUP = 4


def _dot(a, b, dims):
    return lax.dot_general(a, b, (dims, ((), ())), preferred_element_type=F32)


def _stack_heads(x, lane_lo):
    return jnp.concatenate([jnp.where(lane_lo, x, 0.0), jnp.where(lane_lo, 0.0, x)], axis=0).astype(BF16)


def _mixer_kernel(sink_ref, rkv_ref, act_ref, q_ref, kvp_ref, kvc_ref, wl_ref, w0_ref, a0_ref, kk_ref, ka_ref,
                  rk_ref, lnw_ref, lnb_ref, seg_ref, o_ref, oa_ref, s_ref):
    t = pl.program_id(1)
    TB = rkv_ref.shape[1]
    L = CHUNK
    W = RWKV_WIDTH

    @pl.when(t == 0)
    def _():
        s_ref[...] = jnp.zeros_like(s_ref)

    p = rkv_ref[0]
    r = p[:, 0:W]
    k = p[:, W:2 * W]
    v = p[:, 2 * W:3 * W]
    z = _dot(act_ref[0], wl_ref[...], NN)
    e = EXP_M05 * _sigmoid(z[:, 0:W] + w0_ref[...])
    a_ic = _sigmoid(z[:, W:2 * W] + a0_ref[...])
    gate = z[:, 2 * W:3 * W]

    seg = seg_ref[...]
    kk = k * kk_ref[...]
    kk = kk * lax.rsqrt(jnp.maximum(_head_sums(kk * kk, seg), 1e-24))
    k2 = k * (1.0 + (a_ic - 1.0) * ka_ref[...])
    bv = kk * a_ic

    ci = lax.broadcasted_iota(jnp.int32, (MXU_TILE, MXU_TILE), 0)
    cj = lax.broadcasted_iota(jnp.int32, (MXU_TILE, MXU_TILE), 1)
    tri = jnp.where((cj <= ci) & ((ci // L) == (cj // L)), 1.0, 0.0).astype(BF16)
    cw = -jnp.concatenate([_mm_exact_lhs(tri, e[i:i + MXU_TILE], CUMSUM_PASSES)
                           for i in range(0, TB, MXU_TILE)], axis=0)
    at = -kk * jnp.exp(cw + e)
    rt = r * jnp.exp(cw)
    einv = jnp.exp(-cw)
    bt = bv * einv
    kt = k2 * einv

    ti = lax.broadcasted_iota(jnp.int32, (L, PAIR), 0)
    si = lax.broadcasted_iota(jnp.int32, (L, PAIR), 1) % HEAD_DIM
    strict = si < ti
    incl = si <= ti
    eye2 = jnp.where(si == ti, 1.0, 0.0)
    lane_lo = lax.broadcasted_iota(jnp.int32, (1, PAIR), 1) < HEAD_DIM
    bi = lax.broadcasted_iota(jnp.int32, (PAIR, PAIR), 0) // HEAD_DIM
    bj = lax.broadcasted_iota(jnp.int32, (PAIR, PAIR), 1) // HEAD_DIM
    same_head = bi == bj

    n_chunks = TB // L
    rows = lambda c: slice(c * L, (c + 1) * L)
    lanes = lambda pr: slice(pr * PAIR, (pr + 1) * PAIR)
    inv_n = 1.0 / HEAD_DIM

    def prepare(chunks, out):
        units = [(c, pr) for c in chunks for pr in range(N_PAIRS)]
        w_end, eend = {}, {}
        for c in chunks:
            cw_c = cw[rows(c)]
            cw_last = cw_c[L - 1:L, :]
            eend[c] = jnp.exp(cw_last - cw_c)
            w_end[c] = jnp.exp(cw_last)
        a_st, v_st, ab, ak, rb, rk, bkw = {}, {}, {}, {}, {}, {}, {}
        for un in units:
            c, pr = un
            rc, lp = rows(c), lanes(pr)
            ar = jnp.concatenate([at[rc, lp], rt[rc, lp]], axis=0).astype(BF16)
            bk_s = jnp.concatenate([_stack_heads(bt[rc, lp], lane_lo),
                                    _stack_heads(kt[rc, lp], lane_lo)], axis=0)
            sc = _dot(ar, bk_s, NT)
            ab[un] = jnp.where(strict, sc[0:L, 0:PAIR], 0.0)
            ak[un] = jnp.where(strict, sc[0:L, PAIR:2 * PAIR], 0.0)
            rb[un] = jnp.where(incl, sc[L:2 * L, 0:PAIR], 0.0)
            rk[un] = jnp.where(incl, sc[L:2 * L, PAIR:2 * PAIR], 0.0)
            a_st[un] = _stack_heads(at[rc, lp], lane_lo)
            v_st[un] = _stack_heads(v[rc, lp], lane_lo)
            bkw[un] = jnp.concatenate([bv[rc, lp] * eend[c][:, lp], k2[rc, lp] * eend[c][:, lp]],
                                      axis=0).astype(BF16)
        yield
        tm = {un: eye2 + ab[un] for un in units}
        pw = {un: _dot(ab[un].astype(BF16), _stack_heads(ab[un], lane_lo), NN) for un in units}
        yield
        for it in range(5):
            for un in units:
                pw_s = _stack_heads(pw[un], lane_lo)
                if it < 4:
                    both = _dot(jnp.concatenate([tm[un], pw[un]], axis=0).astype(BF16), pw_s, NN)
                    tm[un] = tm[un] + both[0:L]
                    pw[un] = both[L:2 * L]
                else:
                    tm[un] = tm[un] + _dot(tm[un].astype(BF16), pw_s, NN)
            yield
        m2 = {un: _dot(jnp.concatenate([ak[un], rk[un]], axis=0).astype(BF16), v_st[un], NN) for un in units}
        yield
        tau = {un: _dot(tm[un].astype(BF16),
                        jnp.concatenate([a_st[un], _stack_heads(m2[un][0:L], lane_lo)], axis=1), NN)
               for un in units}
        yield
        for un in units:
            c, pr = un
            ap, u0 = tau[un][:, 0:PAIR], tau[un][:, PAIR:2 * PAIR]
            ry = _dot(rb[un].astype(BF16),
                      jnp.concatenate([_stack_heads(ap, lane_lo), _stack_heads(u0, lane_lo)], axis=1), NN)
            out[un] = dict(rp=(rt[rows(c), lanes(pr)] + ry[:, 0:PAIR]).astype(BF16),
                           y0=ry[:, PAIR:2 * PAIR] + m2[un][L:2 * L], w_end=w_end[c][:, lanes(pr)])
        yield
        for un in units:
            c, pr = un
            ap, u0 = tau[un][:, 0:PAIR], tau[un][:, PAIR:2 * PAIR]
            out[un]["mp"] = jnp.where(same_head, _dot(ap.astype(BF16), bkw[un][0:L], TN), 0.0).astype(BF16)
            uv = jnp.concatenate([u0, v[rows(c), lanes(pr)]], axis=0).astype(BF16)
            out[un]["cc"] = jnp.where(same_head, _dot(uv, bkw[un], TN), 0.0)

    def recur(chunks, prep, state, y_rows):
        for c in chunks:
            y_cols = []
            for pr in range(N_PAIRS):
                u = prep[(c, pr)]
                sb = state[pr].astype(BF16)
                y_cols.append(_dot(u["rp"], sb, NT) + u["y0"])
                state[pr] = state[pr] * u["w_end"] + _dot(sb, u["mp"], NN) + u["cc"]
            y_rows[c] = jnp.concatenate(y_cols, axis=1)
            yield

    def finish(chunks, y_rows):
        rg = slice(chunks[0] * L, (chunks[-1] + 1) * L)
        y = jnp.concatenate([y_rows[c] for c in chunks], axis=0)
        mean = _head_sums(y, seg) * inv_n
        yield
        d = y - mean
        var = _head_sums(d * d, seg) * inv_n
        yield
        yn = d * lax.rsqrt(var + GN_EPS) * lnw_ref[...] + lnb_ref[...]
        bonus = _head_sums(r[rg] * k2[rg] * rk_ref[...], seg) * v[rg]
        o_ref[0, rg, :] = ((yn + bonus) * gate[rg]).astype(o_ref.dtype)

    def interleave(*gens):
        gens = list(gens)
        while gens:
            for g in list(gens):
                if next(g, StopIteration) is StopIteration:
                    gens.remove(g)

    groups = [list(range(g * RWKV_GROUP, (g + 1) * RWKV_GROUP)) for g in range(n_chunks // RWKV_GROUP)]
    state = [s_ref[pr] for pr in range(N_PAIRS)]
    y_rows = {}
    preps = [dict() for _ in groups]
    attn = _attn_stages(sink_ref, q_ref, kvp_ref, kvc_ref, oa_ref, t)
    interleave(prepare(groups[0], preps[0]), attn)
    for gi, chunks in enumerate(groups):
        partners = [recur(chunks, preps[gi], state, y_rows), attn]
        if gi + 1 < len(groups):
            partners.append(prepare(groups[gi + 1], preps[gi + 1]))
        if gi > 0:
            partners.append(finish(groups[gi - 1], y_rows))
        interleave(*partners)
    interleave(finish(groups[-1], y_rows), attn)
    for pr in range(N_PAIRS):
        s_ref[pr] = state[pr]


def _mixers(p_rkv, act, p_att, sinks, w_lora, w0, a0, k_k, k_a, r_k, ln_w, ln_b, seg):
    b, t, _ = p_rkv.shape
    W = RWKV_WIDTH
    TB = RWKV_BLOCK
    T = WINDOW
    kvw = 4 * KV_WIDTH
    nq = ATT_WIDTH // kvw
    row = lambda a: a.reshape(1, -1)
    const = lambda a: pl.BlockSpec(a.shape, lambda i, j: (0,) * a.ndim)
    blk = lambda width: pl.BlockSpec((1, TB, width), lambda i, j: (i, j, 0))
    consts = [w_lora, row(w0), row(a0), row(k_k), row(k_a), row(r_k), row(ln_w), row(ln_b), seg]
    return pl.pallas_call(
        _mixer_kernel,
        grid=(b, t // TB),
        in_specs=[pl.BlockSpec(memory_space=pltpu.SMEM), blk(3 * W), blk(LORA_PAD), blk(ATT_WIDTH),
                  pl.BlockSpec((1, T, kvw), lambda i, j: (i, jnp.maximum(j * (TB // T) - 1, 0), nq)),
                  pl.BlockSpec((1, TB, kvw), lambda i, j: (i, j, nq))] + [const(a) for a in consts],
        out_specs=[blk(W), blk(ATT_WIDTH)],
        out_shape=[jax.ShapeDtypeStruct((b, t, W), BF16), jax.ShapeDtypeStruct((b, t, ATT_WIDTH), BF16)],
        scratch_shapes=[pltpu.VMEM((N_PAIRS, PAIR, PAIR), F32)],
        compiler_params=pltpu.CompilerParams(
            dimension_semantics=("parallel", "arbitrary"), vmem_limit_bytes=VMEM_LIMIT),
        name="mixers",
    )(sinks * LOG2E, p_rkv, act, p_att, p_att, p_att, *consts)


LOG2E = 1.4426950408889634


def _attn_stages(sink_ref, q_ref, kvp_ref, kvc_ref, o_ref, blk):
    QB = q_ref.shape[1]
    T = WINDOW
    q = q_ref[0]
    kv = jnp.concatenate([kvp_ref[0], kvc_ref[0]], axis=0)
    kd = [kv[:, g * KV_WIDTH:(g + 1) * KV_WIDTH] for g in range(ATT_KV_HEADS)]
    vd = [kv[:, (ATT_KV_HEADS + g) * KV_WIDTH:(ATT_KV_HEADS + g + 1) * KV_WIDTH] for g in range(ATT_KV_HEADS)]
    lane_lo = lax.broadcasted_iota(jnp.int32, (1, KV_WIDTH), 1) < HEAD_DIM
    zero = jnp.zeros((), q.dtype)

    qi = lax.broadcasted_iota(jnp.int32, (T, 2 * T), 0)
    kj = lax.broadcasted_iota(jnp.int32, (T, 2 * T), 1)
    band = (kj <= qi + T) & (kj > qi + T - WINDOW)
    first = band & ((blk > 0) | (kj >= T))

    for i in range(QB // T):
        rq = slice(i * T, (i + 1) * T)
        rk = slice(i * T, (i + 2) * T)
        mask = first if i == 0 else band
        scores = []
        for g in range(ATT_KV_HEADS):
            parts = []
            for pr in (2 * g, 2 * g + 1):
                q_p = q[rq, pr * KV_WIDTH:(pr + 1) * KV_WIDTH]
                parts += [jnp.where(lane_lo, q_p, zero), jnp.where(lane_lo, zero, q_p)]
            lhs = jnp.concatenate(parts, axis=0)
            scores.append(_dot(lhs, kd[g][rk], NT))
        yield
        outs = []
        for g in range(ATT_KV_HEADS):
            probs = []
            for hh in range(ATT_GROUP):
                s = jnp.where(mask, scores[g][hh * T:(hh + 1) * T], NEG_BIG)
                sink = sink_ref[g * ATT_GROUP + hh]
                m = jnp.maximum(jnp.max(s, axis=-1, keepdims=True), sink)
                pe = jnp.exp2(s - m)
                denom = jnp.sum(pe, axis=-1, keepdims=True) + jnp.exp2(sink - m)
                probs.append((pe * (1.0 / denom)).astype(BF16))
            o = _dot(jnp.concatenate(probs, axis=0), vd[g][rk], NN)
            outs += [jnp.where(lane_lo, o[0:T], o[T:2 * T]), jnp.where(lane_lo, o[2 * T:3 * T], o[3 * T:4 * T])]
            yield
        o_ref[0, rq, :] = jnp.concatenate(outs, axis=1).astype(o_ref.dtype)


def _head_ones():
    i = jnp.arange(MXU_TILE) // HEAD_DIM
    return (i[:, None] == i[None, :]).astype(BF16)


def _layer(x, ffn1_norm, ffn1_w_gate, ffn1_w_up, ffn1_w_down, mix_norm, w_in,
           rwkv_mu, rwkv_w0, rwkv_w_lora_up, rwkv_a0, rwkv_a_lora_up, rwkv_g_lora_up,
           rwkv_k_k, rwkv_k_a, rwkv_r_k, rwkv_ln_w, rwkv_ln_b,
           attn_q_norm, attn_k_norm, attn_sinks,
           w_branch_rwkv, w_branch_attn, w_out,
           ffn2_norm, ffn2_w_gate, ffn2_w_up, ffn2_w_down, final_norm):
    b, t, d = x.shape
    n = b * t
    W = RWKV_WIDTH
    n_lora = DECAY_LORA + ICLR_LORA + GATE_LORA
    rwkv_cols = 3 * W + n_lora
    att_cols = ATT_WIDTH + 2 * KV_WIDTH

    bf = lambda a: a.astype(BF16)
    x2 = x.reshape(n, d)
    x2 = _ffn(x2, ffn1_norm, bf(ffn1_w_gate), bf(ffn1_w_up), bf(ffn1_w_down), tm=FFN_TILE)

    w_groups = (bf(w_in[:, :3 * W]),
                jnp.pad(bf(w_in[:, 3 * W:rwkv_cols]), ((0, 0), (0, LORA_PAD - n_lora))),
                bf(w_in[:, rwkv_cols:rwkv_cols + att_cols]),
                bf(w_in[:, rwkv_cols + att_cols:]))
    mu_rkv = rwkv_mu[:3 * W]
    mu_lora = jnp.concatenate([rwkv_mu[3 * W:], jnp.zeros((LORA_PAD - n_lora,), F32)])
    qn = jnp.tile(attn_q_norm, ATT_Q_HEADS) * (HEAD_DIM ** -0.5 * LOG2E)
    kn = jnp.tile(attn_k_norm, ATT_KV_HEADS)
    seg = _head_ones()
    p_rkv, act, p_att, p_gate = _proj(x2, mix_norm, w_groups, mu_rkv, mu_lora, qn, kn, seg, seq_len=t)

    w_lora = jnp.zeros((LORA_PAD, 3 * W), F32)
    w_lora = w_lora.at[0:DECAY_LORA, 0:W].set(rwkv_w_lora_up)
    w_lora = w_lora.at[DECAY_LORA:DECAY_LORA + ICLR_LORA, W:2 * W].set(rwkv_a_lora_up)
    w_lora = w_lora.at[DECAY_LORA + ICLR_LORA:n_lora, 2 * W:3 * W].set(rwkv_g_lora_up)
    y_rwkv, y_att = _mixers(p_rkv.reshape(b, t, 3 * W), act.reshape(b, t, LORA_PAD),
                            p_att.reshape(b, t, ATT_COLS_OUT), attn_sinks, bf(w_lora),
                            rwkv_w0, rwkv_a0, rwkv_k_k, rwkv_k_a, rwkv_r_k.reshape(-1), rwkv_ln_w, rwkv_ln_b, seg)

    merge_args = (y_rwkv.reshape(n, W), y_att.reshape(n, ATT_WIDTH), p_gate,
                  bf(w_branch_rwkv), bf(w_branch_attn), bf(w_out))
    x2 = _ffn(x2, ffn2_norm, bf(ffn2_w_gate), bf(ffn2_w_up), bf(ffn2_w_down), final_norm, merge_args,
              tm=MERGE_FFN_TILE)
    return x2.reshape(b, t, d)


def kernel(x, ffn1_norm, ffn1_w_gate, ffn1_w_up, ffn1_w_down, mix_norm, w_in, rwkv_mu, rwkv_w0, rwkv_w_lora_up, rwkv_a0, rwkv_a_lora_up, rwkv_g_lora_up, rwkv_k_k, rwkv_k_a, rwkv_r_k, rwkv_ln_w, rwkv_ln_b, attn_q_norm, attn_k_norm, attn_sinks, w_branch_rwkv, w_branch_attn, w_out, ffn2_norm, ffn2_w_gate, ffn2_w_up, ffn2_w_down, final_norm):
    params = (ffn1_norm, ffn1_w_gate, ffn1_w_up, ffn1_w_down, mix_norm, w_in, rwkv_mu, rwkv_w0,
              rwkv_w_lora_up, rwkv_a0, rwkv_a_lora_up, rwkv_g_lora_up, rwkv_k_k, rwkv_k_a, rwkv_r_k,
              rwkv_ln_w, rwkv_ln_b, attn_q_norm, attn_k_norm, attn_sinks, w_branch_rwkv, w_branch_attn,
              w_out, ffn2_norm, ffn2_w_gate, ffn2_w_up, ffn2_w_down, final_norm)
    for layer in range(ffn1_norm.shape[0]):
        x = _layer(x, *(p[layer] for p in params))
    return x
```

```python
import functools

import jax
import jax.numpy as jnp
from jax import lax
from jax.experimental import pallas as pl
from jax.experimental.pallas import tpu as pltpu

F32 = jnp.float32
BF16 = jnp.bfloat16

HEAD_DIM = 64
RWKV_HEADS = 8
RWKV_WIDTH = RWKV_HEADS * HEAD_DIM
ATT_Q_HEADS = 8
ATT_KV_HEADS = 2
ATT_GROUP = ATT_Q_HEADS // ATT_KV_HEADS
ATT_WIDTH = ATT_Q_HEADS * HEAD_DIM
KV_WIDTH = ATT_KV_HEADS * HEAD_DIM
WINDOW = 128
DECAY_LORA = 32
ICLR_LORA = 32
GATE_LORA = 96
LORA_PAD = 256
RMS_EPS = 1e-6
GN_EPS = 64e-5
CHUNK = 64
EXP_M05 = 0.6065306597126334
NEG_BIG = -1e30

MXU_TILE = 256
VMEM_LIMIT = 56 * 1024 * 1024
FFN_TILE = 1024
FFN_ROW_BLOCKS = 4
MERGE_ROW_BLOCKS = 2
MERGE_FFN_TILE = 1024


def _sigmoid(x):
    return 0.5 * jnp.tanh(0.5 * x) + 0.5


NN = ((1,), (0,))
NT = ((1,), (1,))
TN = ((0,), (0,))


HEAD_SUM_TERMS = 1


def _head_sums(x, seg, terms=HEAD_SUM_TERMS):
    rows, cols = x.shape
    width = min(cols, seg.shape[0])
    groups = cols // width
    parts, rem = [], x
    for i in range(terms):
        parts.append(rem.astype(BF16))
        if i + 1 < terms:
            rem = rem - parts[-1].astype(F32)
    stacked = jnp.concatenate([p[:, g * width:(g + 1) * width] for p in parts for g in range(groups)], axis=0)
    out = lax.dot_general(stacked, seg[0:width, 0:width], (NN, ((), ())), preferred_element_type=F32)
    blk = lambda i: out[i * rows:(i + 1) * rows]
    sums = []
    for g in range(groups):
        acc = blk(g)
        for i in range(1, terms):
            acc = acc + blk(i * groups + g)
        sums.append(acc)
    return jnp.concatenate(sums, axis=1)


def _mm_exact_lhs(a_bf16, b, passes=3):
    dn = (NN, ((), ()))
    out = None
    rem = b
    for _ in range(passes):
        part = rem.astype(BF16)
        term = lax.dot_general(a_bf16, part, dn, preferred_element_type=F32)
        out = term if out is None else out + term
        rem = rem - part.astype(F32)
    return out


def _ffn_kernel(*refs, final, merge):
    refs = list(refs)
    take = lambda k: [refs.pop(0) for _ in range(k)]
    (x_ref,) = take(1)
    if merge:
        yr_ref, ya_ref, gate_ref, wr_ref, wa_ref, wo_ref = take(6)
    g_ref, wg_ref, wu_ref, wd_ref = take(4)
    if final:
        (fg_ref,) = take(1)
    o_ref, h_ref, act_ref = refs

    tm = x_ref.shape[0]
    dff = wg_ref.shape[1]
    rb = tm // (MERGE_ROW_BLOCKS if merge else FFN_ROW_BLOCKS)
    blocks = [slice(r0, r0 + rb) for r0 in range(0, tm, rb)]
    for rs in blocks:
        x = x_ref[rs, :]
        if merge:
            d = x.shape[1]
            gate = gate_ref[rs, :].astype(F32)
            br = jnp.dot(yr_ref[rs, :], wr_ref[...], preferred_element_type=F32)
            ba = jnp.dot(ya_ref[rs, :], wa_ref[...], preferred_element_type=F32)
            merged = _sigmoid(gate[:, 0:d]) * br + _sigmoid(gate[:, d:2 * d]) * ba
            x = x + jnp.dot(merged.astype(BF16), wo_ref[...], preferred_element_type=F32)
            o_ref[rs, :] = x
        ms = jnp.mean(x * x, axis=-1, keepdims=True)
        h_ref[rs, :] = (x * lax.rsqrt(ms + RMS_EPS) * g_ref[...]).astype(BF16)
    for rs in blocks:
        for c in range(dff // MXU_TILE):
            sl = slice(c * MXU_TILE, (c + 1) * MXU_TILE)
            g = jnp.dot(h_ref[rs, :], wg_ref[:, sl], preferred_element_type=F32)
            u = jnp.dot(h_ref[rs, :], wu_ref[:, sl], preferred_element_type=F32)
            act_ref[rs, sl] = (g * _sigmoid(g) * (0.5 * u)).astype(BF16)
    for rs in blocks:
        resid = o_ref[rs, :] if merge else x_ref[rs, :]
        y = resid + jnp.dot(act_ref[rs, :], wd_ref[...], preferred_element_type=F32)
        if final:
            ms = jnp.mean(y * y, axis=-1, keepdims=True)
            y = y * lax.rsqrt(ms + RMS_EPS) * fg_ref[...]
        o_ref[rs, :] = y


def _ffn(x, gain, wg, wu, wd, final_gain=None, merge_args=None, *, tm):
    n, d = x.shape
    dff = wg.shape[1]
    final = final_gain is not None
    merge = merge_args is not None
    tile = lambda a: pl.BlockSpec((tm, a.shape[1]), lambda i: (i, 0))
    row = lambda a: pl.BlockSpec((1, a.shape[-1]), lambda i: (0, 0))
    resident = lambda a: pl.BlockSpec(a.shape, lambda i: (0,) * a.ndim, pipeline_mode=pl.Buffered(1))
    in_specs, args = [tile(x)], [x]
    if merge:
        in_specs += [tile(a) for a in merge_args[:3]] + [resident(a) for a in merge_args[3:]]
        args += list(merge_args)
    in_specs += [row(gain), resident(wg), resident(wu), resident(wd)]
    args += [gain.reshape(1, d), wg, wu, wd]
    if final:
        in_specs.append(row(final_gain))
        args.append(final_gain.reshape(1, d))
    return pl.pallas_call(
        functools.partial(_ffn_kernel, final=final, merge=merge),
        grid=(n // tm,),
        in_specs=in_specs,
        out_specs=tile(x),
        out_shape=jax.ShapeDtypeStruct((n, d), F32),
        scratch_shapes=[pltpu.VMEM((tm, d), BF16), pltpu.VMEM((tm, dff), BF16)],
        compiler_params=pltpu.CompilerParams(
            dimension_semantics=("parallel",), vmem_limit_bytes=VMEM_LIMIT),
        name="merge_ffn" if merge else "ffn",
    )(*args)


def _proj_kernel(x_ref, g_ref, w_rkv_ref, w_lora_ref, w_att_ref, w_gate_ref, mu_rkv_ref, mu_lora_ref,
                 qn_ref, kn_ref, seg_ref, rkv_ref, act_ref, att_ref, gate_ref, prev_rkv_ref, prev_lora_ref,
                 *, tiles_per_seq):
    tm = x_ref.shape[0]
    first = pl.program_id(0) % tiles_per_seq == 0

    @pl.when(first)
    def _():
        prev_rkv_ref[...] = jnp.zeros_like(prev_rkv_ref)
        prev_lora_ref[...] = jnp.zeros_like(prev_lora_ref)

    x = x_ref[...]
    ms = jnp.mean(x * x, axis=-1, keepdims=True)
    h = (x * lax.rsqrt(ms + RMS_EPS) * g_ref[...]).astype(BF16)
    row = lax.broadcasted_iota(jnp.int32, (tm, 1), 0)

    def lerp(p, prev_ref, mu):
        shifted = jnp.where(row == 0, prev_ref[...], pltpu.roll(p, 1, axis=0))
        prev_ref[...] = p[tm - 1:tm, :]
        return p + (shifted - p) * mu

    rkv_ref[...] = lerp(jnp.dot(h, w_rkv_ref[...], preferred_element_type=F32), prev_rkv_ref, mu_rkv_ref[...])

    pl_ = lerp(jnp.dot(h, w_lora_ref[...], preferred_element_type=F32), prev_lora_ref, mu_lora_ref[...])
    lane = lax.broadcasted_iota(jnp.int32, (1, LORA_PAD), 1)
    act = jnp.where(lane < DECAY_LORA, jnp.tanh(pl_),
                    jnp.where(lane < DECAY_LORA + ICLR_LORA, pl_,
                              jnp.where(lane < DECAY_LORA + ICLR_LORA + GATE_LORA, _sigmoid(pl_), 0.0)))
    act_ref[...] = act.astype(act_ref.dtype)

    a = jnp.dot(h, w_att_ref[...], preferred_element_type=F32)
    inv_n = 1.0 / HEAD_DIM
    seg = seg_ref[...]
    q = a[:, 0:ATT_WIDTH]
    k = a[:, ATT_WIDTH:ATT_WIDTH + KV_WIDTH]
    v = a[:, ATT_WIDTH + KV_WIDTH:ATT_WIDTH + 2 * KV_WIDTH]
    q = q * lax.rsqrt(_head_sums(q * q, seg) * inv_n + RMS_EPS) * qn_ref[...]
    k = k * lax.rsqrt(_head_sums(k * k, seg) * inv_n + RMS_EPS) * kn_ref[...]
    lane_lo = lax.broadcasted_iota(jnp.int32, (1, KV_WIDTH), 1) < HEAD_DIM
    k_sw = pltpu.roll(k, HEAD_DIM, axis=1)
    v_sw = pltpu.roll(v, HEAD_DIM, axis=1)
    att_ref[...] = jnp.concatenate(
        [q, jnp.where(lane_lo, k, k_sw), jnp.where(lane_lo, k_sw, k),
         jnp.where(lane_lo, v, v_sw), jnp.where(lane_lo, v_sw, v)], axis=1).astype(att_ref.dtype)

    gate_ref[...] = jnp.dot(h, w_gate_ref[...], preferred_element_type=F32).astype(gate_ref.dtype)


ATT_COLS_OUT = ATT_WIDTH + 4 * KV_WIDTH


def _proj(x, gain, weights, mu_rkv, mu_lora, qn, kn, seg, *, seq_len, tm=1024):
    n, d = x.shape
    W = RWKV_WIDTH
    resident = lambda a: pl.BlockSpec(a.shape, lambda i: (0, 0), pipeline_mode=pl.Buffered(1))
    row = lambda a: a.reshape(1, -1)
    smalls = [row(mu_rkv), row(mu_lora), row(qn), row(kn), seg]
    outs = [(3 * W, F32), (LORA_PAD, BF16), (ATT_COLS_OUT, BF16), (weights[3].shape[1], BF16)]
    return pl.pallas_call(
        functools.partial(_proj_kernel, tiles_per_seq=seq_len // tm),
        grid=(n // tm,),
        in_specs=[pl.BlockSpec((tm, d), lambda i: (i, 0)), pl.BlockSpec((1, d), lambda i: (0, 0))]
        + [resident(w) for w in weights] + [resident(a) for a in smalls],
        out_specs=[pl.BlockSpec((tm, c), lambda i: (i, 0)) for c, _ in outs],
        out_shape=[jax.ShapeDtypeStruct((n, c), dt) for c, dt in outs],
        scratch_shapes=[pltpu.VMEM((1, 3 * W), F32), pltpu.VMEM((1, LORA_PAD), F32)],
        compiler_params=pltpu.CompilerParams(
            dimension_semantics=("arbitrary",), vmem_limit_bytes=VMEM_LIMIT),
        name="proj",
    )(x, gain.reshape(1, d), *weights, *smalls)


PAIR = 2 * HEAD_DIM
N_PAIRS = RWKV_HEADS // 2
RWKV_BLOCK = 512
CUMSUM_PASSES = 2
RWKV_GROUP = 4


def _dot(a, b, dims):
    return lax.dot_general(a, b, (dims, ((), ())), preferred_element_type=F32)


def _stack_heads(x, lane_lo):
    return jnp.concatenate([jnp.where(lane_lo, x, 0.0), jnp.where(lane_lo, 0.0, x)], axis=0).astype(BF16)


def _mixer_kernel(sink_ref, rkv_ref, act_ref, q_ref, kvp_ref, kvc_ref, wl_ref, w0_ref, a0_ref, kk_ref, ka_ref,
                  rk_ref, lnw_ref, lnb_ref, seg_ref, o_ref, oa_ref, s_ref):
    t = pl.program_id(1)
    TB = rkv_ref.shape[1]
    L = CHUNK
    W = RWKV_WIDTH

    @pl.when(t == 0)
    def _():
        s_ref[...] = jnp.zeros_like(s_ref)

    p = rkv_ref[0]
    r = p[:, 0:W]
    k = p[:, W:2 * W]
    v = p[:, 2 * W:3 * W]
    z = _dot(act_ref[0], wl_ref[...], NN)
    e = EXP_M05 * _sigmoid(z[:, 0:W] + w0_ref[...])
    a_ic = _sigmoid(z[:, W:2 * W] + a0_ref[...])
    gate = z[:, 2 * W:3 * W]

    seg = seg_ref[...]
    kk = k * kk_ref[...]
    kk = kk * lax.rsqrt(jnp.maximum(_head_sums(kk * kk, seg), 1e-24))
    k2 = k * (1.0 + (a_ic - 1.0) * ka_ref[...])
    bv = kk * a_ic

    ci = lax.broadcasted_iota(jnp.int32, (MXU_TILE, MXU_TILE), 0)
    cj = lax.broadcasted_iota(jnp.int32, (MXU_TILE, MXU_TILE), 1)
    tri = jnp.where((cj <= ci) & ((ci // L) == (cj // L)), 1.0, 0.0).astype(BF16)
    cw = -jnp.concatenate([_mm_exact_lhs(tri, e[i:i + MXU_TILE], CUMSUM_PASSES)
                           for i in range(0, TB, MXU_TILE)], axis=0)
    at = -kk * jnp.exp(cw + e)
    rt = r * jnp.exp(cw)
    einv = jnp.exp(-cw)
    bt = bv * einv
    kt = k2 * einv

    ti = lax.broadcasted_iota(jnp.int32, (L, PAIR), 0)
    si = lax.broadcasted_iota(jnp.int32, (L, PAIR), 1) % HEAD_DIM
    strict = si < ti
    incl = si <= ti
    eye2 = jnp.where(si == ti, 1.0, 0.0)
    lane_lo = lax.broadcasted_iota(jnp.int32, (1, PAIR), 1) < HEAD_DIM
    bi = lax.broadcasted_iota(jnp.int32, (PAIR, PAIR), 0) // HEAD_DIM
    bj = lax.broadcasted_iota(jnp.int32, (PAIR, PAIR), 1) // HEAD_DIM
    same_head = bi == bj

    n_chunks = TB // L
    rows = lambda c: slice(c * L, (c + 1) * L)
    lanes = lambda pr: slice(pr * PAIR, (pr + 1) * PAIR)
    inv_n = 1.0 / HEAD_DIM

    def prepare(chunks, out):
        units = [(c, pr) for c in chunks for pr in range(N_PAIRS)]
        w_end, eend = {}, {}
        for c in chunks:
            cw_c = cw[rows(c)]
            cw_last = cw_c[L - 1:L, :]
            eend[c] = jnp.exp(cw_last - cw_c)
            w_end[c] = jnp.exp(cw_last)
        a_st, v_st, ab, ak, rb, rk, bkw = {}, {}, {}, {}, {}, {}, {}
        for un in units:
            c, pr = un
            rc, lp = rows(c), lanes(pr)
            ar = jnp.concatenate([at[rc, lp], rt[rc, lp]], axis=0).astype(BF16)
            bk_s = jnp.concatenate([_stack_heads(bt[rc, lp], lane_lo),
                                    _stack_heads(kt[rc, lp], lane_lo)], axis=0)
            sc = _dot(ar, bk_s, NT)
            ab[un] = jnp.where(strict, sc[0:L, 0:PAIR], 0.0)
            ak[un] = jnp.where(strict, sc[0:L, PAIR:2 * PAIR], 0.0)
            rb[un] = jnp.where(incl, sc[L:2 * L, 0:PAIR], 0.0)
            rk[un] = jnp.where(incl, sc[L:2 * L, PAIR:2 * PAIR], 0.0)
            a_st[un] = _stack_heads(at[rc, lp], lane_lo)
            v_st[un] = _stack_heads(v[rc, lp], lane_lo)
            bkw[un] = jnp.concatenate([bv[rc, lp] * eend[c][:, lp], k2[rc, lp] * eend[c][:, lp]],
                                      axis=0).astype(BF16)
        yield
        tm = {un: eye2 + ab[un] for un in units}
        pw = {un: _dot(ab[un].astype(BF16), _stack_heads(ab[un], lane_lo), NN) for un in units}
        yield
        for it in range(5):
            for un in units:
                pw_s = _stack_heads(pw[un], lane_lo)
                if it < 4:
                    both = _dot(jnp.concatenate([tm[un], pw[un]], axis=0).astype(BF16), pw_s, NN)
                    tm[un] = tm[un] + both[0:L]
                    pw[un] = both[L:2 * L]
                else:
                    tm[un] = tm[un] + _dot(tm[un].astype(BF16), pw_s, NN)
            yield
        m2 = {un: _dot(jnp.concatenate([ak[un], rk[un]], axis=0).astype(BF16), v_st[un], NN) for un in units}
        yield
        tau = {un: _dot(tm[un].astype(BF16),
                        jnp.concatenate([a_st[un], _stack_heads(m2[un][0:L], lane_lo)], axis=1), NN)
               for un in units}
        yield
        for un in units:
            c, pr = un
            ap, u0 = tau[un][:, 0:PAIR], tau[un][:, PAIR:2 * PAIR]
            ry = _dot(rb[un].astype(BF16),
                      jnp.concatenate([_stack_heads(ap, lane_lo), _stack_heads(u0, lane_lo)], axis=1), NN)
            out[un] = dict(rp=(rt[rows(c), lanes(pr)] + ry[:, 0:PAIR]).astype(BF16),
                           y0=ry[:, PAIR:2 * PAIR] + m2[un][L:2 * L], w_end=w_end[c][:, lanes(pr)])
        yield
        for un in units:
            c, pr = un
            ap, u0 = tau[un][:, 0:PAIR], tau[un][:, PAIR:2 * PAIR]
            out[un]["mp"] = jnp.where(same_head, _dot(ap.astype(BF16), bkw[un][0:L], TN), 0.0).astype(BF16)
            uv = jnp.concatenate([u0, v[rows(c), lanes(pr)]], axis=0).astype(BF16)
            out[un]["cc"] = jnp.where(same_head, _dot(uv, bkw[un], TN), 0.0)

    def recur(chunks, prep, state, y_rows):
        for c in chunks:
            y_cols = []
            for pr in range(N_PAIRS):
                u = prep[(c, pr)]
                sb = state[pr].astype(BF16)
                y_cols.append(_dot(u["rp"], sb, NT) + u["y0"])
                state[pr] = state[pr] * u["w_end"] + _dot(sb, u["mp"], NN) + u["cc"]
            y_rows[c] = jnp.concatenate(y_cols, axis=1)
            yield

    def finish(chunks, y_rows):
        rg = slice(chunks[0] * L, (chunks[-1] + 1) * L)
        y = jnp.concatenate([y_rows[c] for c in chunks], axis=0)
        mean = _head_sums(y, seg) * inv_n
        yield
        d = y - mean
        var = _head_sums(d * d, seg) * inv_n
        yield
        yn = d * lax.rsqrt(var + GN_EPS) * lnw_ref[...] + lnb_ref[...]
        bonus = _head_sums(r[rg] * k2[rg] * rk_ref[...], seg) * v[rg]
        o_ref[0, rg, :] = ((yn + bonus) * gate[rg]).astype(o_ref.dtype)

    def interleave(*gens):
        gens = list(gens)
        while gens:
            for g in list(gens):
                if next(g, StopIteration) is StopIteration:
                    gens.remove(g)

    groups = [list(range(g * RWKV_GROUP, (g + 1) * RWKV_GROUP)) for g in range(n_chunks // RWKV_GROUP)]
    state = [s_ref[pr] for pr in range(N_PAIRS)]
    y_rows = {}
    preps = [dict() for _ in groups]
    attn = _attn_stages(sink_ref, q_ref, kvp_ref, kvc_ref, oa_ref, t)
    interleave(prepare(groups[0], preps[0]), attn)
    for gi, chunks in enumerate(groups):
        partners = [recur(chunks, preps[gi], state, y_rows), attn]
        if gi + 1 < len(groups):
            partners.append(prepare(groups[gi + 1], preps[gi + 1]))
        if gi > 0:
            partners.append(finish(groups[gi - 1], y_rows))
        interleave(*partners)
    interleave(finish(groups[-1], y_rows), attn)
    for pr in range(N_PAIRS):
        s_ref[pr] = state[pr]


def _mixers(p_rkv, act, p_att, sinks, w_lora, w0, a0, k_k, k_a, r_k, ln_w, ln_b, seg):
    b, t, _ = p_rkv.shape
    W = RWKV_WIDTH
    TB = RWKV_BLOCK
    T = WINDOW
    kvw = 4 * KV_WIDTH
    nq = ATT_WIDTH // kvw
    row = lambda a: a.reshape(1, -1)
    const = lambda a: pl.BlockSpec(a.shape, lambda i, j: (0,) * a.ndim)
    blk = lambda width: pl.BlockSpec((1, TB, width), lambda i, j: (i, j, 0))
    consts = [w_lora, row(w0), row(a0), row(k_k), row(k_a), row(r_k), row(ln_w), row(ln_b), seg]
    return pl.pallas_call(
        _mixer_kernel,
        grid=(b, t // TB),
        in_specs=[pl.BlockSpec(memory_space=pltpu.SMEM), blk(3 * W), blk(LORA_PAD), blk(ATT_WIDTH),
                  pl.BlockSpec((1, T, kvw), lambda i, j: (i, jnp.maximum(j * (TB // T) - 1, 0), nq)),
                  pl.BlockSpec((1, TB, kvw), lambda i, j: (i, j, nq))] + [const(a) for a in consts],
        out_specs=[blk(W), blk(ATT_WIDTH)],
        out_shape=[jax.ShapeDtypeStruct((b, t, W), BF16), jax.ShapeDtypeStruct((b, t, ATT_WIDTH), BF16)],
        scratch_shapes=[pltpu.VMEM((N_PAIRS, PAIR, PAIR), F32)],
        compiler_params=pltpu.CompilerParams(
            dimension_semantics=("parallel", "arbitrary"), vmem_limit_bytes=VMEM_LIMIT),
        name="mixers",
    )(sinks * LOG2E, p_rkv, act, p_att, p_att, p_att, *consts)


LOG2E = 1.4426950408889634


def _attn_stages(sink_ref, q_ref, kvp_ref, kvc_ref, o_ref, blk):
    QB = q_ref.shape[1]
    T = WINDOW
    q = q_ref[0]
    kv = jnp.concatenate([kvp_ref[0], kvc_ref[0]], axis=0)
    kd = [kv[:, g * KV_WIDTH:(g + 1) * KV_WIDTH] for g in range(ATT_KV_HEADS)]
    vd = [kv[:, (ATT_KV_HEADS + g) * KV_WIDTH:(ATT_KV_HEADS + g + 1) * KV_WIDTH] for g in range(ATT_KV_HEADS)]
    lane_lo = lax.broadcasted_iota(jnp.int32, (1, KV_WIDTH), 1) < HEAD_DIM
    zero = jnp.zeros((), q.dtype)

    qi = lax.broadcasted_iota(jnp.int32, (T, 2 * T), 0)
    kj = lax.broadcasted_iota(jnp.int32, (T, 2 * T), 1)
    band = (kj <= qi + T) & (kj > qi + T - WINDOW)
    first = band & ((blk > 0) | (kj >= T))

    for i in range(QB // T):
        rq = slice(i * T, (i + 1) * T)
        rk = slice(i * T, (i + 2) * T)
        mask = first if i == 0 else band
        scores = []
        for g in range(ATT_KV_HEADS):
            parts = []
            for pr in (2 * g, 2 * g + 1):
                q_p = q[rq, pr * KV_WIDTH:(pr + 1) * KV_WIDTH]
                parts += [jnp.where(lane_lo, q_p, zero), jnp.where(lane_lo, zero, q_p)]
            lhs = jnp.concatenate(parts, axis=0)
            scores.append(_dot(lhs, kd[g][rk], NT))
        yield
        outs = []
        for g in range(ATT_KV_HEADS):
            probs = []
            for hh in range(ATT_GROUP):
                s = jnp.where(mask, scores[g][hh * T:(hh + 1) * T], NEG_BIG)
                sink = sink_ref[g * ATT_GROUP + hh]
                m = jnp.maximum(jnp.max(s, axis=-1, keepdims=True), sink)
                pe = jnp.exp2(s - m)
                denom = jnp.sum(pe, axis=-1, keepdims=True) + jnp.exp2(sink - m)
                probs.append((pe * (1.0 / denom)).astype(BF16))
            o = _dot(jnp.concatenate(probs, axis=0), vd[g][rk], NN)
            outs += [jnp.where(lane_lo, o[0:T], o[T:2 * T]), jnp.where(lane_lo, o[2 * T:3 * T], o[3 * T:4 * T])]
            yield
        o_ref[0, rq, :] = jnp.concatenate(outs, axis=1).astype(o_ref.dtype)


def _head_ones():
    i = jnp.arange(MXU_TILE) // HEAD_DIM
    return (i[:, None] == i[None, :]).astype(BF16)


def _layer(x, ffn1_norm, ffn1_w_gate, ffn1_w_up, ffn1_w_down, mix_norm, w_in,
           rwkv_mu, rwkv_w0, rwkv_w_lora_up, rwkv_a0, rwkv_a_lora_up, rwkv_g_lora_up,
           rwkv_k_k, rwkv_k_a, rwkv_r_k, rwkv_ln_w, rwkv_ln_b,
           attn_q_norm, attn_k_norm, attn_sinks,
           w_branch_rwkv, w_branch_attn, w_out,
           ffn2_norm, ffn2_w_gate, ffn2_w_up, ffn2_w_down, final_norm):
    b, t, d = x.shape
    n = b * t
    W = RWKV_WIDTH
    n_lora = DECAY_LORA + ICLR_LORA + GATE_LORA
    rwkv_cols = 3 * W + n_lora
    att_cols = ATT_WIDTH + 2 * KV_WIDTH

    bf = lambda a: a.astype(BF16)
    x2 = x.reshape(n, d)
    x2 = _ffn(x2, ffn1_norm, bf(ffn1_w_gate), bf(ffn1_w_up), bf(ffn1_w_down), tm=FFN_TILE)

    w_groups = (bf(w_in[:, :3 * W]),
                jnp.pad(bf(w_in[:, 3 * W:rwkv_cols]), ((0, 0), (0, LORA_PAD - n_lora))),
                bf(w_in[:, rwkv_cols:rwkv_cols + att_cols]),
                bf(w_in[:, rwkv_cols + att_cols:]))
    mu_rkv = rwkv_mu[:3 * W]
    mu_lora = jnp.concatenate([rwkv_mu[3 * W:], jnp.zeros((LORA_PAD - n_lora,), F32)])
    qn = jnp.tile(attn_q_norm, ATT_Q_HEADS) * (HEAD_DIM ** -0.5 * LOG2E)
    kn = jnp.tile(attn_k_norm, ATT_KV_HEADS)
    seg = _head_ones()
    p_rkv, act, p_att, p_gate = _proj(x2, mix_norm, w_groups, mu_rkv, mu_lora, qn, kn, seg, seq_len=t)

    w_lora = jnp.zeros((LORA_PAD, 3 * W), F32)
    w_lora = w_lora.at[0:DECAY_LORA, 0:W].set(rwkv_w_lora_up)
    w_lora = w_lora.at[DECAY_LORA:DECAY_LORA + ICLR_LORA, W:2 * W].set(rwkv_a_lora_up)
    w_lora = w_lora.at[DECAY_LORA + ICLR_LORA:n_lora, 2 * W:3 * W].set(rwkv_g_lora_up)
    y_rwkv, y_att = _mixers(p_rkv.reshape(b, t, 3 * W), act.reshape(b, t, LORA_PAD),
                            p_att.reshape(b, t, ATT_COLS_OUT), attn_sinks, bf(w_lora),
                            rwkv_w0, rwkv_a0, rwkv_k_k, rwkv_k_a, rwkv_r_k.reshape(-1), rwkv_ln_w, rwkv_ln_b, seg)

    merge_args = (y_rwkv.reshape(n, W), y_att.reshape(n, ATT_WIDTH), p_gate,
                  bf(w_branch_rwkv), bf(w_branch_attn), bf(w_out))
    x2 = _ffn(x2, ffn2_norm, bf(ffn2_w_gate), bf(ffn2_w_up), bf(ffn2_w_down), final_norm, merge_args,
              tm=MERGE_FFN_TILE)
    return x2.reshape(b, t, d)


def kernel(x, ffn1_norm, ffn1_w_gate, ffn1_w_up, ffn1_w_down, mix_norm, w_in, rwkv_mu, rwkv_w0, rwkv_w_lora_up, rwkv_a0, rwkv_a_lora_up, rwkv_g_lora_up, rwkv_k_k, rwkv_k_a, rwkv_r_k, rwkv_ln_w, rwkv_ln_b, attn_q_norm, attn_k_norm, attn_sinks, w_branch_rwkv, w_branch_attn, w_out, ffn2_norm, ffn2_w_gate, ffn2_w_up, ffn2_w_down, final_norm):
    params = (ffn1_norm, ffn1_w_gate, ffn1_w_up, ffn1_w_down, mix_norm, w_in, rwkv_mu, rwkv_w0,
              rwkv_w_lora_up, rwkv_a0, rwkv_a_lora_up, rwkv_g_lora_up, rwkv_k_k, rwkv_k_a, rwkv_r_k,
              rwkv_ln_w, rwkv_ln_b, attn_q_norm, attn_k_norm, attn_sinks, w_branch_rwkv, w_branch_attn,
              w_out, ffn2_norm, ffn2_w_gate, ffn2_w_up, ffn2_w_down, final_norm)
    for layer in range(ffn1_norm.shape[0]):
        x = _layer(x, *(p[layer] for p in params))
    return x
```

```python
import functools

import jax
import jax.numpy as jnp
from jax import lax
from jax.experimental import pallas as pl
from jax.experimental.pallas import tpu as pltpu

F32 = jnp.float32
BF16 = jnp.bfloat16

HEAD_DIM = 64
RWKV_HEADS = 8
RWKV_WIDTH = RWKV_HEADS * HEAD_DIM
ATT_Q_HEADS = 8
ATT_KV_HEADS = 2
ATT_GROUP = ATT_Q_HEADS // ATT_KV_HEADS
ATT_WIDTH = ATT_Q_HEADS * HEAD_DIM
KV_WIDTH = ATT_KV_HEADS * HEAD_DIM
WINDOW = 128
DECAY_LORA = 32
ICLR_LORA = 32
GATE_LORA = 96
LORA_PAD = 256
RMS_EPS = 1e-6
GN_EPS = 64e-5
CHUNK = 64
EXP_M05 = 0.6065306597126334
NEG_BIG = -1e30

MXU_TILE = 256
VMEM_LIMIT = 56 * 1024 * 1024
FFN_TILE = 1024
FFN_ROW_BLOCKS = 4
MERGE_ROW_BLOCKS = 2
MERGE_FFN_TILE = 1024


def _sigmoid(x):
    return 0.5 * jnp.tanh(0.5 * x) + 0.5


NN = ((1,), (0,))
NT = ((1,), (1,))
TN = ((0,), (0,))


HEAD_SUM_TERMS = 1


def _head_sums(x, seg, terms=HEAD_SUM_TERMS):
    rows, cols = x.shape
    width = min(cols, seg.shape[0])
    groups = cols // width
    parts, rem = [], x
    for i in range(terms):
        parts.append(rem.astype(BF16))
        if i + 1 < terms:
            rem = rem - parts[-1].astype(F32)
    stacked = jnp.concatenate([p[:, g * width:(g + 1) * width] for p in parts for g in range(groups)], axis=0)
    out = lax.dot_general(stacked, seg[0:width, 0:width], (NN, ((), ())), preferred_element_type=F32)
    blk = lambda i: out[i * rows:(i + 1) * rows]
    sums = []
    for g in range(groups):
        acc = blk(g)
        for i in range(1, terms):
            acc = acc + blk(i * groups + g)
        sums.append(acc)
    return jnp.concatenate(sums, axis=1)


def _mm_exact_lhs(a_bf16, b, passes=3):
    dn = (NN, ((), ()))
    out = None
    rem = b
    for _ in range(passes):
        part = rem.astype(BF16)
        term = lax.dot_general(a_bf16, part, dn, preferred_element_type=F32)
        out = term if out is None else out + term
        rem = rem - part.astype(F32)
    return out


def _ffn_kernel(*refs, final, merge):
    refs = list(refs)
    take = lambda k: [refs.pop(0) for _ in range(k)]
    (x_ref,) = take(1)
    if merge:
        yr_ref, ya_ref, gate_ref, wr_ref, wa_ref, wo_ref = take(6)
    g_ref, wg_ref, wu_ref, wd_ref = take(4)
    if final:
        (fg_ref,) = take(1)
    o_ref, h_ref, act_ref = refs

    tm = x_ref.shape[0]
    dff = wg_ref.shape[1]
    rb = tm // (MERGE_ROW_BLOCKS if merge else FFN_ROW_BLOCKS)
    blocks = [slice(r0, r0 + rb) for r0 in range(0, tm, rb)]
    for rs in blocks:
        x = x_ref[rs, :]
        if merge:
            d = x.shape[1]
            gate = gate_ref[rs, :].astype(F32)
            br = jnp.dot(yr_ref[rs, :], wr_ref[...], preferred_element_type=F32)
            ba = jnp.dot(ya_ref[rs, :], wa_ref[...], preferred_element_type=F32)
            merged = _sigmoid(gate[:, 0:d]) * br + _sigmoid(gate[:, d:2 * d]) * ba
            x = x + jnp.dot(merged.astype(BF16), wo_ref[...], preferred_element_type=F32)
            o_ref[rs, :] = x
        ms = jnp.mean(x * x, axis=-1, keepdims=True)
        h_ref[rs, :] = (x * lax.rsqrt(ms + RMS_EPS) * g_ref[...]).astype(BF16)
    for rs in blocks:
        for c in range(dff // MXU_TILE):
            sl = slice(c * MXU_TILE, (c + 1) * MXU_TILE)
            g = jnp.dot(h_ref[rs, :], wg_ref[:, sl], preferred_element_type=F32)
            u = jnp.dot(h_ref[rs, :], wu_ref[:, sl], preferred_element_type=F32)
            act_ref[rs, sl] = (g * _sigmoid(g) * (0.5 * u)).astype(BF16)
    for rs in blocks:
        resid = o_ref[rs, :] if merge else x_ref[rs, :]
        y = resid + jnp.dot(act_ref[rs, :], wd_ref[...], preferred_element_type=F32)
        if final:
            ms = jnp.mean(y * y, axis=-1, keepdims=True)
            y = y * lax.rsqrt(ms + RMS_EPS) * fg_ref[...]
        o_ref[rs, :] = y


def _ffn(x, gain, wg, wu, wd, final_gain=None, merge_args=None, *, tm):
    n, d = x.shape
    dff = wg.shape[1]
    final = final_gain is not None
    merge = merge_args is not None
    tile = lambda a: pl.BlockSpec((tm, a.shape[1]), lambda i: (i, 0))
    row = lambda a: pl.BlockSpec((1, a.shape[-1]), lambda i: (0, 0))
    resident = lambda a: pl.BlockSpec(a.shape, lambda i: (0,) * a.ndim, pipeline_mode=pl.Buffered(1))
    in_specs, args = [tile(x)], [x]
    if merge:
        in_specs += [tile(a) for a in merge_args[:3]] + [resident(a) for a in merge_args[3:]]
        args += list(merge_args)
    in_specs += [row(gain), resident(wg), resident(wu), resident(wd)]
    args += [gain.reshape(1, d), wg, wu, wd]
    if final:
        in_specs.append(row(final_gain))
        args.append(final_gain.reshape(1, d))
    return pl.pallas_call(
        functools.partial(_ffn_kernel, final=final, merge=merge),
        grid=(n // tm,),
        in_specs=in_specs,
        out_specs=tile(x),
        out_shape=jax.ShapeDtypeStruct((n, d), F32),
        scratch_shapes=[pltpu.VMEM((tm, d), BF16), pltpu.VMEM((tm, dff), BF16)],
        compiler_params=pltpu.CompilerParams(
            dimension_semantics=("parallel",), vmem_limit_bytes=VMEM_LIMIT),
        name="merge_ffn" if merge else "ffn",
    )(*args)


def _proj_kernel(x_ref, g_ref, w_rkv_ref, w_lora_ref, w_att_ref, w_gate_ref, mu_rkv_ref, mu_lora_ref,
                 qn_ref, kn_ref, seg_ref, rkv_ref, act_ref, att_ref, gate_ref, prev_rkv_ref, prev_lora_ref,
                 *, tiles_per_seq):
    tm = x_ref.shape[0]
    first = pl.program_id(0) % tiles_per_seq == 0

    @pl.when(first)
    def _():
        prev_rkv_ref[...] = jnp.zeros_like(prev_rkv_ref)
        prev_lora_ref[...] = jnp.zeros_like(prev_lora_ref)

    x = x_ref[...]
    ms = jnp.mean(x * x, axis=-1, keepdims=True)
    h = (x * lax.rsqrt(ms + RMS_EPS) * g_ref[...]).astype(BF16)
    row = lax.broadcasted_iota(jnp.int32, (tm, 1), 0)

    def lerp(p, prev_ref, mu):
        shifted = jnp.where(row == 0, prev_ref[...], pltpu.roll(p, 1, axis=0))
        prev_ref[...] = p[tm - 1:tm, :]
        return p + (shifted - p) * mu

    mm = lambda w_ref: lax.dot_general(h, w_ref[...], (NT, ((), ())), preferred_element_type=F32)
    rkv_ref[...] = lerp(mm(w_rkv_ref), prev_rkv_ref, mu_rkv_ref[...])

    pl_ = lerp(mm(w_lora_ref), prev_lora_ref, mu_lora_ref[...])
    lane = lax.broadcasted_iota(jnp.int32, (1, LORA_PAD), 1)
    act = jnp.where(lane < DECAY_LORA, jnp.tanh(pl_),
                    jnp.where(lane < DECAY_LORA + ICLR_LORA, pl_,
                              jnp.where(lane < DECAY_LORA + ICLR_LORA + GATE_LORA, _sigmoid(pl_), 0.0)))
    act_ref[...] = act.astype(act_ref.dtype)

    a = mm(w_att_ref)
    inv_n = 1.0 / HEAD_DIM
    seg = seg_ref[...]
    q = a[:, 0:ATT_WIDTH]
    k = a[:, ATT_WIDTH:ATT_WIDTH + KV_WIDTH]
    v = a[:, ATT_WIDTH + KV_WIDTH:ATT_WIDTH + 2 * KV_WIDTH]
    q = q * lax.rsqrt(_head_sums(q * q, seg) * inv_n + RMS_EPS) * qn_ref[...]
    k = k * lax.rsqrt(_head_sums(k * k, seg) * inv_n + RMS_EPS) * kn_ref[...]
    lane_lo = lax.broadcasted_iota(jnp.int32, (1, KV_WIDTH), 1) < HEAD_DIM
    k_sw = pltpu.roll(k, HEAD_DIM, axis=1)
    v_sw = pltpu.roll(v, HEAD_DIM, axis=1)
    att_ref[...] = jnp.concatenate(
        [q, jnp.where(lane_lo, k, k_sw), jnp.where(lane_lo, k_sw, k),
         jnp.where(lane_lo, v, v_sw), jnp.where(lane_lo, v_sw, v)], axis=1).astype(att_ref.dtype)

    gate_ref[...] = mm(w_gate_ref).astype(gate_ref.dtype)


ATT_COLS_OUT = ATT_WIDTH + 4 * KV_WIDTH


def _proj(x, gain, weights, mu_rkv, mu_lora, qn, kn, seg, *, seq_len, tm=1024):
    n, d = x.shape
    W = RWKV_WIDTH
    resident = lambda a: pl.BlockSpec(a.shape, lambda i: (0, 0), pipeline_mode=pl.Buffered(1))
    row = lambda a: a.reshape(1, -1)
    smalls = [row(mu_rkv), row(mu_lora), row(qn), row(kn), seg]
    outs = [(3 * W, F32), (LORA_PAD, BF16), (ATT_COLS_OUT, BF16), (weights[3].shape[0], BF16)]
    return pl.pallas_call(
        functools.partial(_proj_kernel, tiles_per_seq=seq_len // tm),
        grid=(n // tm,),
        in_specs=[pl.BlockSpec((tm, d), lambda i: (i, 0)), pl.BlockSpec((1, d), lambda i: (0, 0))]
        + [resident(w) for w in weights] + [resident(a) for a in smalls],
        out_specs=[pl.BlockSpec((tm, c), lambda i: (i, 0)) for c, _ in outs],
        out_shape=[jax.ShapeDtypeStruct((n, c), dt) for c, dt in outs],
        scratch_shapes=[pltpu.VMEM((1, 3 * W), F32), pltpu.VMEM((1, LORA_PAD), F32)],
        compiler_params=pltpu.CompilerParams(
            dimension_semantics=("arbitrary",), vmem_limit_bytes=VMEM_LIMIT),
        name="proj",
    )(x, gain.reshape(1, d), *weights, *smalls)


PAIR = 2 * HEAD_DIM
N_PAIRS = RWKV_HEADS // 2
RWKV_BLOCK = 512
CUMSUM_PASSES = 2
RWKV_GROUP = 4


def _dot(a, b, dims):
    return lax.dot_general(a, b, (dims, ((), ())), preferred_element_type=F32)


def _stack_heads(x, lane_lo):
    return jnp.concatenate([jnp.where(lane_lo, x, 0.0), jnp.where(lane_lo, 0.0, x)], axis=0).astype(BF16)


def _mixer_kernel(sink_ref, rkv_ref, act_ref, q_ref, kvp_ref, kvc_ref, wl_ref, w0_ref, a0_ref, kk_ref, ka_ref,
                  rk_ref, lnw_ref, lnb_ref, seg_ref, o_ref, oa_ref, s_ref):
    t = pl.program_id(1)
    TB = rkv_ref.shape[1]
    L = CHUNK
    W = RWKV_WIDTH

    @pl.when(t == 0)
    def _():
        s_ref[...] = jnp.zeros_like(s_ref)

    p = rkv_ref[0]
    r = p[:, 0:W]
    k = p[:, W:2 * W]
    v = p[:, 2 * W:3 * W]
    z = _dot(act_ref[0], wl_ref[...], NN)
    e = EXP_M05 * _sigmoid(z[:, 0:W] + w0_ref[...])
    a_ic = _sigmoid(z[:, W:2 * W] + a0_ref[...])
    gate = z[:, 2 * W:3 * W]

    seg = seg_ref[...]
    kk = k * kk_ref[...]
    kk = kk * lax.rsqrt(jnp.maximum(_head_sums(kk * kk, seg), 1e-24))
    k2 = k * (1.0 + (a_ic - 1.0) * ka_ref[...])
    bv = kk * a_ic

    ci = lax.broadcasted_iota(jnp.int32, (MXU_TILE, MXU_TILE), 0)
    cj = lax.broadcasted_iota(jnp.int32, (MXU_TILE, MXU_TILE), 1)
    tri = jnp.where((cj <= ci) & ((ci // L) == (cj // L)), 1.0, 0.0).astype(BF16)
    cw = -jnp.concatenate([_mm_exact_lhs(tri, e[i:i + MXU_TILE], CUMSUM_PASSES)
                           for i in range(0, TB, MXU_TILE)], axis=0)
    at = -kk * jnp.exp(cw + e)
    rt = r * jnp.exp(cw)
    einv = jnp.exp(-cw)
    bt = bv * einv
    kt = k2 * einv

    ti = lax.broadcasted_iota(jnp.int32, (L, PAIR), 0)
    si = lax.broadcasted_iota(jnp.int32, (L, PAIR), 1) % HEAD_DIM
    strict = si < ti
    incl = si <= ti
    eye2 = jnp.where(si == ti, 1.0, 0.0)
    lane_lo = lax.broadcasted_iota(jnp.int32, (1, PAIR), 1) < HEAD_DIM
    bi = lax.broadcasted_iota(jnp.int32, (PAIR, PAIR), 0) // HEAD_DIM
    bj = lax.broadcasted_iota(jnp.int32, (PAIR, PAIR), 1) // HEAD_DIM
    same_head = bi == bj

    n_chunks = TB // L
    rows = lambda c: slice(c * L, (c + 1) * L)
    lanes = lambda pr: slice(pr * PAIR, (pr + 1) * PAIR)
    inv_n = 1.0 / HEAD_DIM

    def prepare(chunks, out):
        units = [(c, pr) for c in chunks for pr in range(N_PAIRS)]
        w_end, eend = {}, {}
        for c in chunks:
            cw_c = cw[rows(c)]
            cw_last = cw_c[L - 1:L, :]
            eend[c] = jnp.exp(cw_last - cw_c)
            w_end[c] = jnp.exp(cw_last)
        a_st, v_st, ab, ak, rb, rk, bkw = {}, {}, {}, {}, {}, {}, {}
        for un in units:
            c, pr = un
            rc, lp = rows(c), lanes(pr)
            ar = jnp.concatenate([at[rc, lp], rt[rc, lp]], axis=0).astype(BF16)
            bk_s = jnp.concatenate([_stack_heads(bt[rc, lp], lane_lo),
                                    _stack_heads(kt[rc, lp], lane_lo)], axis=0)
            sc = _dot(ar, bk_s, NT)
            ab[un] = jnp.where(strict, sc[0:L, 0:PAIR], 0.0)
            ak[un] = jnp.where(strict, sc[0:L, PAIR:2 * PAIR], 0.0)
            rb[un] = jnp.where(incl, sc[L:2 * L, 0:PAIR], 0.0)
            rk[un] = jnp.where(incl, sc[L:2 * L, PAIR:2 * PAIR], 0.0)
            a_st[un] = _stack_heads(at[rc, lp], lane_lo)
            v_st[un] = _stack_heads(v[rc, lp], lane_lo)
            bkw[un] = jnp.concatenate([bv[rc, lp] * eend[c][:, lp], k2[rc, lp] * eend[c][:, lp]],
                                      axis=0).astype(BF16)
        yield
        tm = {un: eye2 + ab[un] for un in units}
        pw = {un: _dot(ab[un].astype(BF16), _stack_heads(ab[un], lane_lo), NN) for un in units}
        yield
        for it in range(5):
            for un in units:
                pw_s = _stack_heads(pw[un], lane_lo)
                if it < 4:
                    both = _dot(jnp.concatenate([tm[un], pw[un]], axis=0).astype(BF16), pw_s, NN)
                    tm[un] = tm[un] + both[0:L]
                    pw[un] = both[L:2 * L]
                else:
                    tm[un] = tm[un] + _dot(tm[un].astype(BF16), pw_s, NN)
            yield
        m2 = {un: _dot(jnp.concatenate([ak[un], rk[un]], axis=0).astype(BF16), v_st[un], NN) for un in units}
        yield
        tau = {un: _dot(tm[un].astype(BF16),
                        jnp.concatenate([a_st[un], _stack_heads(m2[un][0:L], lane_lo)], axis=1), NN)
               for un in units}
        yield
        for un in units:
            c, pr = un
            ap, u0 = tau[un][:, 0:PAIR], tau[un][:, PAIR:2 * PAIR]
            ry = _dot(rb[un].astype(BF16),
                      jnp.concatenate([_stack_heads(ap, lane_lo), _stack_heads(u0, lane_lo)], axis=1), NN)
            out[un] = dict(rp=(rt[rows(c), lanes(pr)] + ry[:, 0:PAIR]).astype(BF16),
                           y0=ry[:, PAIR:2 * PAIR] + m2[un][L:2 * L], w_end=w_end[c][:, lanes(pr)])
        yield
        for un in units:
            c, pr = un
            ap, u0 = tau[un][:, 0:PAIR], tau[un][:, PAIR:2 * PAIR]
            out[un]["mp"] = jnp.where(same_head, _dot(ap.astype(BF16), bkw[un][0:L], TN), 0.0).astype(BF16)
            uv = jnp.concatenate([u0, v[rows(c), lanes(pr)]], axis=0).astype(BF16)
            out[un]["cc"] = jnp.where(same_head, _dot(uv, bkw[un], TN), 0.0)

    def recur(chunks, prep, state, y_rows):
        for c in chunks:
            y_cols = []
            for pr in range(N_PAIRS):
                u = prep[(c, pr)]
                sb = state[pr].astype(BF16)
                y_cols.append(_dot(u["rp"], sb, NT) + u["y0"])
                state[pr] = state[pr] * u["w_end"] + _dot(sb, u["mp"], NN) + u["cc"]
            y_rows[c] = jnp.concatenate(y_cols, axis=1)
            yield

    def finish(chunks, y_rows):
        rg = slice(chunks[0] * L, (chunks[-1] + 1) * L)
        y = jnp.concatenate([y_rows[c] for c in chunks], axis=0)
        mean = _head_sums(y, seg) * inv_n
        yield
        d = y - mean
        var = _head_sums(d * d, seg) * inv_n
        yield
        yn = d * lax.rsqrt(var + GN_EPS) * lnw_ref[...] + lnb_ref[...]
        bonus = _head_sums(r[rg] * k2[rg] * rk_ref[...], seg) * v[rg]
        o_ref[0, rg, :] = ((yn + bonus) * gate[rg]).astype(o_ref.dtype)

    def interleave(*gens):
        gens = list(gens)
        while gens:
            for g in list(gens):
                if next(g, StopIteration) is StopIteration:
                    gens.remove(g)

    groups = [list(range(g * RWKV_GROUP, (g + 1) * RWKV_GROUP)) for g in range(n_chunks // RWKV_GROUP)]
    state = [s_ref[pr] for pr in range(N_PAIRS)]
    y_rows = {}
    preps = [dict() for _ in groups]
    attn = _attn_stages(sink_ref, q_ref, kvp_ref, kvc_ref, oa_ref, t)
    interleave(prepare(groups[0], preps[0]), attn)
    for gi, chunks in enumerate(groups):
        partners = [recur(chunks, preps[gi], state, y_rows), attn]
        if gi + 1 < len(groups):
            partners.append(prepare(groups[gi + 1], preps[gi + 1]))
        if gi > 0:
            partners.append(finish(groups[gi - 1], y_rows))
        interleave(*partners)
    interleave(finish(groups[-1], y_rows), attn)
    for pr in range(N_PAIRS):
        s_ref[pr] = state[pr]


def _mixers(p_rkv, act, p_att, sinks, w_lora, w0, a0, k_k, k_a, r_k, ln_w, ln_b, seg):
    b, t, _ = p_rkv.shape
    W = RWKV_WIDTH
    TB = RWKV_BLOCK
    T = WINDOW
    kvw = 4 * KV_WIDTH
    nq = ATT_WIDTH // kvw
    row = lambda a: a.reshape(1, -1)
    const = lambda a: pl.BlockSpec(a.shape, lambda i, j: (0,) * a.ndim)
    blk = lambda width: pl.BlockSpec((1, TB, width), lambda i, j: (i, j, 0))
    consts = [w_lora, row(w0), row(a0), row(k_k), row(k_a), row(r_k), row(ln_w), row(ln_b), seg]
    return pl.pallas_call(
        _mixer_kernel,
        grid=(b, t // TB),
        in_specs=[pl.BlockSpec(memory_space=pltpu.SMEM), blk(3 * W), blk(LORA_PAD), blk(ATT_WIDTH),
                  pl.BlockSpec((1, T, kvw), lambda i, j: (i, jnp.maximum(j * (TB // T) - 1, 0), nq)),
                  pl.BlockSpec((1, TB, kvw), lambda i, j: (i, j, nq))] + [const(a) for a in consts],
        out_specs=[blk(W), blk(ATT_WIDTH)],
        out_shape=[jax.ShapeDtypeStruct((b, t, W), BF16), jax.ShapeDtypeStruct((b, t, ATT_WIDTH), BF16)],
        scratch_shapes=[pltpu.VMEM((N_PAIRS, PAIR, PAIR), F32)],
        compiler_params=pltpu.CompilerParams(
            dimension_semantics=("parallel", "arbitrary"), vmem_limit_bytes=VMEM_LIMIT),
        name="mixers",
    )(sinks * LOG2E, p_rkv, act, p_att, p_att, p_att, *consts)


LOG2E = 1.4426950408889634


def _attn_stages(sink_ref, q_ref, kvp_ref, kvc_ref, o_ref, blk):
    QB = q_ref.shape[1]
    T = WINDOW
    q = q_ref[0]
    kv = jnp.concatenate([kvp_ref[0], kvc_ref[0]], axis=0)
    kd = [kv[:, g * KV_WIDTH:(g + 1) * KV_WIDTH] for g in range(ATT_KV_HEADS)]
    vd = [kv[:, (ATT_KV_HEADS + g) * KV_WIDTH:(ATT_KV_HEADS + g + 1) * KV_WIDTH] for g in range(ATT_KV_HEADS)]
    lane_lo = lax.broadcasted_iota(jnp.int32, (1, KV_WIDTH), 1) < HEAD_DIM
    zero = jnp.zeros((), q.dtype)

    qi = lax.broadcasted_iota(jnp.int32, (T, 2 * T), 0)
    kj = lax.broadcasted_iota(jnp.int32, (T, 2 * T), 1)
    band = (kj <= qi + T) & (kj > qi + T - WINDOW)
    first = band & ((blk > 0) | (kj >= T))

    for i in range(QB // T):
        rq = slice(i * T, (i + 1) * T)
        rk = slice(i * T, (i + 2) * T)
        mask = first if i == 0 else band
        scores = []
        for g in range(ATT_KV_HEADS):
            parts = []
            for pr in (2 * g, 2 * g + 1):
                q_p = q[rq, pr * KV_WIDTH:(pr + 1) * KV_WIDTH]
                parts += [jnp.where(lane_lo, q_p, zero), jnp.where(lane_lo, zero, q_p)]
            lhs = jnp.concatenate(parts, axis=0)
            scores.append(_dot(lhs, kd[g][rk], NT))
        yield
        outs = []
        for g in range(ATT_KV_HEADS):
            probs = []
            for hh in range(ATT_GROUP):
                s = jnp.where(mask, scores[g][hh * T:(hh + 1) * T], NEG_BIG)
                sink = sink_ref[g * ATT_GROUP + hh]
                m = jnp.maximum(jnp.max(s, axis=-1, keepdims=True), sink)
                pe = jnp.exp2(s - m)
                denom = jnp.sum(pe, axis=-1, keepdims=True) + jnp.exp2(sink - m)
                probs.append((pe * (1.0 / denom)).astype(BF16))
            o = _dot(jnp.concatenate(probs, axis=0), vd[g][rk], NN)
            outs += [jnp.where(lane_lo, o[0:T], o[T:2 * T]), jnp.where(lane_lo, o[2 * T:3 * T], o[3 * T:4 * T])]
            yield
        o_ref[0, rq, :] = jnp.concatenate(outs, axis=1).astype(o_ref.dtype)


def _head_ones():
    i = jnp.arange(MXU_TILE) // HEAD_DIM
    return (i[:, None] == i[None, :]).astype(BF16)


def _layer(x, ffn1_norm, ffn1_w_gate, ffn1_w_up, ffn1_w_down, mix_norm, w_in,
           rwkv_mu, rwkv_w0, rwkv_w_lora_up, rwkv_a0, rwkv_a_lora_up, rwkv_g_lora_up,
           rwkv_k_k, rwkv_k_a, rwkv_r_k, rwkv_ln_w, rwkv_ln_b,
           attn_q_norm, attn_k_norm, attn_sinks,
           w_branch_rwkv, w_branch_attn, w_out,
           ffn2_norm, ffn2_w_gate, ffn2_w_up, ffn2_w_down, final_norm):
    b, t, d = x.shape
    n = b * t
    W = RWKV_WIDTH
    n_lora = DECAY_LORA + ICLR_LORA + GATE_LORA
    rwkv_cols = 3 * W + n_lora
    att_cols = ATT_WIDTH + 2 * KV_WIDTH

    bf = lambda a: a.astype(BF16)
    x2 = x.reshape(n, d)
    x2 = _ffn(x2, ffn1_norm, bf(ffn1_w_gate), bf(ffn1_w_up), bf(ffn1_w_down), tm=FFN_TILE)

    w_t = w_in.T
    w_groups = (bf(w_t[:3 * W]),
                jnp.pad(bf(w_t[3 * W:rwkv_cols]), ((0, LORA_PAD - n_lora), (0, 0))),
                bf(w_t[rwkv_cols:rwkv_cols + att_cols]),
                bf(w_t[rwkv_cols + att_cols:]))
    mu_rkv = rwkv_mu[:3 * W]
    mu_lora = jnp.concatenate([rwkv_mu[3 * W:], jnp.zeros((LORA_PAD - n_lora,), F32)])
    qn = jnp.tile(attn_q_norm, ATT_Q_HEADS) * (HEAD_DIM ** -0.5 * LOG2E)
    kn = jnp.tile(attn_k_norm, ATT_KV_HEADS)
    seg = _head_ones()
    p_rkv, act, p_att, p_gate = _proj(x2, mix_norm, w_groups, mu_rkv, mu_lora, qn, kn, seg, seq_len=t)

    w_lora = jnp.zeros((LORA_PAD, 3 * W), F32)
    w_lora = w_lora.at[0:DECAY_LORA, 0:W].set(rwkv_w_lora_up)
    w_lora = w_lora.at[DECAY_LORA:DECAY_LORA + ICLR_LORA, W:2 * W].set(rwkv_a_lora_up)
    w_lora = w_lora.at[DECAY_LORA + ICLR_LORA:n_lora, 2 * W:3 * W].set(rwkv_g_lora_up)
    y_rwkv, y_att = _mixers(p_rkv.reshape(b, t, 3 * W), act.reshape(b, t, LORA_PAD),
                            p_att.reshape(b, t, ATT_COLS_OUT), attn_sinks, bf(w_lora),
                            rwkv_w0, rwkv_a0, rwkv_k_k, rwkv_k_a, rwkv_r_k.reshape(-1), rwkv_ln_w, rwkv_ln_b, seg)

    merge_args = (y_rwkv.reshape(n, W), y_att.reshape(n, ATT_WIDTH), p_gate,
                  bf(w_branch_rwkv), bf(w_branch_attn), bf(w_out))
    x2 = _ffn(x2, ffn2_norm, bf(ffn2_w_gate), bf(ffn2_w_up), bf(ffn2_w_down), final_norm, merge_args,
              tm=MERGE_FFN_TILE)
    return x2.reshape(b, t, d)


def kernel(x, ffn1_norm, ffn1_w_gate, ffn1_w_up, ffn1_w_down, mix_norm, w_in, rwkv_mu, rwkv_w0, rwkv_w_lora_up, rwkv_a0, rwkv_a_lora_up, rwkv_g_lora_up, rwkv_k_k, rwkv_k_a, rwkv_r_k, rwkv_ln_w, rwkv_ln_b, attn_q_norm, attn_k_norm, attn_sinks, w_branch_rwkv, w_branch_attn, w_out, ffn2_norm, ffn2_w_gate, ffn2_w_up, ffn2_w_down, final_norm):
    params = (ffn1_norm, ffn1_w_gate, ffn1_w_up, ffn1_w_down, mix_norm, w_in, rwkv_mu, rwkv_w0,
              rwkv_w_lora_up, rwkv_a0, rwkv_a_lora_up, rwkv_g_lora_up, rwkv_k_k, rwkv_k_a, rwkv_r_k,
              rwkv_ln_w, rwkv_ln_b, attn_q_norm, attn_k_norm, attn_sinks, w_branch_rwkv, w_branch_attn,
              w_out, ffn2_norm, ffn2_w_gate, ffn2_w_up, ffn2_w_down, final_norm)
    for layer in range(ffn1_norm.shape[0]):
        x = _layer(x, *(p[layer] for p in params))
    return x
```
